```python
import jax, jax.numpy as jnp
from jax import lax
import numpy as np

D_MODEL = 1024
BATCH = 8
SEQ = 8192
DEPTH = 2
DEC_BATCH = 16
DEC_SEQ = 64
PAST_LEN = 2048

CHUNK = 64
N_MIXERS = 2
N_LRU_LAYERS = (DEPTH + 1) // 2
N_RWKV_LAYERS = DEPTH // 2
LRU_WIDTH = D_MODEL
LRU_BLOCKS = 8
LRU_BLOCK_DIM = LRU_WIDTH // LRU_BLOCKS
LRU_CONV = 4
LRU_C = 8.0
RWKV_HEAD = 64
RWKV_HEADS = D_MODEL // RWKV_HEAD
DECAY_LORA = 64
AAA_LORA = 64
GATE_LORA = 128
D_FF = 3 * D_MODEL
FFN_CONV = 3
RMS_EPS = 1e-6
GN_EPS = 64e-5

kernel_name = 'hybrid_rglru_rwkv7_convffn_stream_step'


def rmsnorm(x, g):
    x32 = x.astype(jnp.float32)
    y = x32 * lax.rsqrt(jnp.mean(x32 * x32, axis=-1, keepdims=True) + RMS_EPS)
    return (y * g.astype(jnp.float32)).astype(x.dtype)


def causal_dwconv(x, buf, w, b):
    k = w.shape[0]
    t = x.shape[1]
    xp = jnp.concatenate([buf.astype(x.dtype), x], axis=1)
    y = b + xp[:, k - 1:k - 1 + t] * w[k - 1]
    for j in range(k - 1):
        y = y + xp[:, j:j + t] * w[j]
    return y, xp[:, t:]


def block_diag(u, w, b):
    bsz, t, _ = u.shape
    ub = u.reshape(bsz, t, LRU_BLOCKS, LRU_BLOCK_DIM)
    return (jnp.einsum('btnc,ncd->btnd', ub, w) + b).reshape(bsz, t, LRU_WIDTH)


def lru_mixer(h, conv_buf, h0, w_x, b_x, w_y, b_y, conv_w, conv_b, ga_w, ga_b, gx_w, gx_b, lam, w_out, b_out):
    gate = jax.nn.gelu(h @ w_y + b_y)
    u, new_buf = causal_dwconv(h @ w_x + b_x, conv_buf, conv_w, conv_b)
    r = jax.nn.sigmoid(block_diag(u, ga_w, ga_b)).astype(jnp.float32)
    i = jax.nn.sigmoid(block_diag(u, gx_w, gx_b))
    log_a = -LRU_C * r * jax.nn.softplus(-lam.astype(jnp.float32))
    a = jnp.exp(log_a)
    bx = jnp.sqrt(-jnp.expm1(2.0 * log_a)) * (i * u).astype(jnp.float32)
    bx = bx.at[:, 0].add(a[:, 0] * h0.astype(jnp.float32))

    def combine(left, right):
        a1, b1 = left
        a2, b2 = right
        return a1 * a2, a2 * b1 + b2

    _, hs = lax.associative_scan(combine, (a, bx), axis=1)
    y = (hs.astype(h.dtype) * gate) @ w_out + b_out
    return y, new_buf, hs[:, -1]


def rwkv7_mixer(h, shift_prev, s0, mu, w_r, w_k, w_v, w_o, w0, w1, w2, a0, a1, a2, g1, g2, k_k, k_a, r_k, ln_w, ln_b):
    bsz, t, c = h.shape
    f32 = jnp.float32
    prev = jnp.concatenate([shift_prev[:, None].astype(h.dtype), h[:, :-1]], axis=1)
    xx = prev - h
    xr, xw, xk, xv, xa, xg = [h + xx * mu[j] for j in range(6)]
    r = xr @ w_r
    k = xk @ w_k
    v = xv @ w_v
    wlog = -jax.nn.softplus(-(w0 + jnp.tanh(xw @ w1) @ w2).astype(f32)) - 0.5
    decay = jnp.exp(-jnp.exp(wlog))
    a = jax.nn.sigmoid(a0 + (xa @ a1) @ a2)
    g = jax.nn.sigmoid(xg @ g1) @ g2
    heads = lambda z: z.astype(f32).reshape(bsz, t, RWKV_HEADS, RWKV_HEAD)
    kk = heads(k * k_k)
    kk = kk / jnp.maximum(jnp.sqrt(jnp.sum(kk * kk, axis=-1, keepdims=True)), 1e-12)
    k = k * (1.0 + (a - 1.0) * k_a)
    r_h, k_h, v_h, w_h, a_h = heads(r), heads(k), heads(v), heads(decay), heads(a)
    tm = lambda z: jnp.swapaxes(z, 0, 1)

    def step(s, inp):
        r_t, w_t, k_t, v_t, aa_t, bb_t = inp
        sa = jnp.einsum('bhij,bhj->bhi', s, aa_t)
        s = s * w_t[:, :, None, :] + sa[..., None] * bb_t[:, :, None, :] + v_t[..., None] * k_t[:, :, None, :]
        return s, jnp.einsum('bhij,bhj->bhi', s, r_t)

    s_last, ys = lax.scan(step, s0.astype(f32),
                          (tm(r_h), tm(w_h), tm(k_h), tm(v_h), tm(-kk), tm(kk * a_h)))
    y = tm(ys)
    mean = jnp.mean(y, axis=-1, keepdims=True)
    var = jnp.mean(jnp.square(y - mean), axis=-1, keepdims=True)
    yn = ((y - mean) * lax.rsqrt(var + GN_EPS)).reshape(bsz, t, c) * ln_w.astype(f32) + ln_b.astype(f32)
    bonus = (jnp.sum(r_h * k_h * r_k.astype(f32), axis=-1, keepdims=True) * v_h).reshape(bsz, t, c)
    out = ((yn + bonus).astype(h.dtype) * g) @ w_o
    return out, h[:, -1], s_last


def conv_ffn(h, buf, w_up, conv_w, conv_b, w_down):
    u, new_buf = causal_dwconv(h @ w_up, buf, conv_w, conv_b)
    g, val = jnp.split(u, 2, axis=-1)
    return (jax.nn.gelu(g) * val) @ w_down, new_buf


def trunk(x, lru_conv, lru_h, rw_shift, rw_wkv, ffn_conv, norm_mix, norm_ffn, norm_final, lru_p, rw_p, ffn_p):
    new_lc, new_lh, new_rs, new_rw, new_fc = [], [], [], [], []
    i_lru = 0
    i_rw = 0
    for layer in range(DEPTH):
        hn = rmsnorm(x, norm_mix[layer])
        if layer % N_MIXERS == 0:
            y, cb, hl = lru_mixer(hn, lru_conv[i_lru], lru_h[i_lru], *[w[i_lru] for w in lru_p])
            new_lc.append(cb)
            new_lh.append(hl)
            i_lru += 1
        else:
            y, sh, sw = rwkv7_mixer(hn, rw_shift[i_rw], rw_wkv[i_rw], *[w[i_rw] for w in rw_p])
            new_rs.append(sh)
            new_rw.append(sw)
            i_rw += 1
        x = x + y
        hn = rmsnorm(x, norm_ffn[layer])
        y, fb = conv_ffn(hn, ffn_conv[layer], *[w[layer] for w in ffn_p])
        new_fc.append(fb)
        x = x + y
    st = lambda lst: jnp.stack(lst).astype(x.dtype)
    return rmsnorm(x, norm_final), st(new_lc), st(new_lh), st(new_rs), st(new_rw), st(new_fc)


def setup_inputs(seed: int = 0) -> dict:
    key = jax.random.key(seed)
    cnt = [0]

    def nk():
        cnt[0] += 1
        return jax.random.fold_in(key, cnt[0])

    nrm = lambda shape, scale: scale * jax.random.normal(nk(), shape, jnp.float32)
    unif = lambda shape, lo, hi: jax.random.uniform(nk(), shape, jnp.float32, lo, hi)
    NA, NB, L = N_LRU_LAYERS, N_RWKV_LAYERS, DEPTH
    D, LW, H, N = D_MODEL, LRU_WIDTH, RWKV_HEADS, RWKV_HEAD
    LB, BD = LRU_BLOCKS, LRU_BLOCK_DIM
    s = unif((NA, LW), 0.9, 0.999) ** (1.0 / LRU_C)
    lam = jnp.log(s) - jnp.log1p(-s)
    return {
        'x_prompt': nrm((BATCH, SEQ, D), 1.0),
        'x_sample': nrm((DEC_BATCH, DEC_SEQ, D), 1.0),
        'state_lru_conv': nrm((NA, DEC_BATCH, LRU_CONV - 1, LW), 1.0),
        'state_lru_h': nrm((NA, DEC_BATCH, LW), 0.5),
        'state_rwkv_shift': nrm((NB, DEC_BATCH, D), 1.0),
        'state_rwkv_wkv': nrm((NB, DEC_BATCH, H, N, N), 0.3),
        'state_ffn_conv': nrm((L, DEC_BATCH, FFN_CONV - 1, 2 * D_FF), 1.0),
        'norm_mix': 1.0 + nrm((L, D), 0.05),
        'norm_ffn': 1.0 + nrm((L, D), 0.05),
        'norm_final': 1.0 + nrm((D,), 0.05),
        'lru_w_x': nrm((NA, D, LW), D ** -0.5),
        'lru_b_x': nrm((NA, LW), 0.02),
        'lru_w_y': nrm((NA, D, LW), D ** -0.5),
        'lru_b_y': nrm((NA, LW), 0.02),
        'lru_conv_w': nrm((NA, LRU_CONV, LW), LRU_CONV ** -0.5),
        'lru_conv_b': nrm((NA, LW), 0.02),
        'lru_ga_w': nrm((NA, LB, BD, BD), BD ** -0.5),
        'lru_ga_b': nrm((NA, LB, BD), 0.02),
        'lru_gx_w': nrm((NA, LB, BD, BD), BD ** -0.5),
        'lru_gx_b': nrm((NA, LB, BD), 0.02),
        'lru_lambda': lam,
        'lru_w_out': nrm((NA, LW, D), LW ** -0.5),
        'lru_b_out': nrm((NA, D), 0.02),
        'rw_mu': unif((NB, 6, D), 0.0, 1.0),
        'rw_w_r': nrm((NB, D, D), D ** -0.5),
        'rw_w_k': nrm((NB, D, D), D ** -0.5),
        'rw_w_v': nrm((NB, D, D), D ** -0.5),
        'rw_w_o': nrm((NB, D, D), D ** -0.5),
        'rw_w0': unif((NB, D), -6.0, -1.0),
        'rw_w1': nrm((NB, D, DECAY_LORA), D ** -0.5),
        'rw_w2': nrm((NB, DECAY_LORA, D), 0.1 * DECAY_LORA ** -0.5),
        'rw_a0': nrm((NB, D), 0.1),
        'rw_a1': nrm((NB, D, AAA_LORA), D ** -0.5),
        'rw_a2': nrm((NB, AAA_LORA, D), 0.1 * AAA_LORA ** -0.5),
        'rw_g1': nrm((NB, D, GATE_LORA), D ** -0.5),
        'rw_g2': nrm((NB, GATE_LORA, D), GATE_LORA ** -0.5),
        'rw_k_k': 0.85 + nrm((NB, D), 0.05),
        'rw_k_a': 1.0 + nrm((NB, D), 0.05),
        'rw_r_k': nrm((NB, H, N), 0.1),
        'rw_ln_w': 1.0 + nrm((NB, D), 0.05),
        'rw_ln_b': nrm((NB, D), 0.02),
        'ffn_w_up': nrm((L, D, 2 * D_FF), D ** -0.5),
        'ffn_conv_w': nrm((L, FFN_CONV, 2 * D_FF), FFN_CONV ** -0.5),
        'ffn_conv_b': nrm((L, 2 * D_FF), 0.02),
        'ffn_w_down': nrm((L, D_FF, D), D_FF ** -0.5),
    }


def reference(x_prompt, x_sample, state_lru_conv, state_lru_h, state_rwkv_shift, state_rwkv_wkv, state_ffn_conv,
              norm_mix, norm_ffn, norm_final,
              lru_w_x, lru_b_x, lru_w_y, lru_b_y, lru_conv_w, lru_conv_b, lru_ga_w, lru_ga_b, lru_gx_w, lru_gx_b,
              lru_lambda, lru_w_out, lru_b_out,
              rw_mu, rw_w_r, rw_w_k, rw_w_v, rw_w_o, rw_w0, rw_w1, rw_w2, rw_a0, rw_a1, rw_a2, rw_g1, rw_g2,
              rw_k_k, rw_k_a, rw_r_k, rw_ln_w, rw_ln_b,
              ffn_w_up, ffn_conv_w, ffn_conv_b, ffn_w_down):
    lru_p = (lru_w_x, lru_b_x, lru_w_y, lru_b_y, lru_conv_w, lru_conv_b, lru_ga_w, lru_ga_b,
             lru_gx_w, lru_gx_b, lru_lambda, lru_w_out, lru_b_out)
    rw_p = (rw_mu, rw_w_r, rw_w_k, rw_w_v, rw_w_o, rw_w0, rw_w1, rw_w2, rw_a0, rw_a1, rw_a2,
            rw_g1, rw_g2, rw_k_k, rw_k_a, rw_r_k, rw_ln_w, rw_ln_b)
    ffn_p = (ffn_w_up, ffn_conv_w, ffn_conv_b, ffn_w_down)
    bp = x_prompt.shape[0]
    z = lambda shape: jnp.zeros(shape, jnp.float32)
    y_prompt, p_lc, p_lh, p_rs, p_rw, p_fc = trunk(
        x_prompt,
        z((N_LRU_LAYERS, bp, LRU_CONV - 1, LRU_WIDTH)), z((N_LRU_LAYERS, bp, LRU_WIDTH)),
        z((N_RWKV_LAYERS, bp, D_MODEL)), z((N_RWKV_LAYERS, bp, RWKV_HEADS, RWKV_HEAD, RWKV_HEAD)),
        z((DEPTH, bp, FFN_CONV - 1, 2 * D_FF)),
        norm_mix, norm_ffn, norm_final, lru_p, rw_p, ffn_p)
    y_sample, s_lc, s_lh, s_rs, s_rw, s_fc = trunk(
        x_sample, state_lru_conv, state_lru_h, state_rwkv_shift, state_rwkv_wkv, state_ffn_conv,
        norm_mix, norm_ffn, norm_final, lru_p, rw_p, ffn_p)
    return (y_prompt, y_sample, p_lc, p_lh, p_rs, p_rw, p_fc, s_lc, s_lh, s_rs, s_rw, s_fc)
```

```python
import functools

import jax
import jax.numpy as jnp
from jax import lax
from jax.experimental import pallas as pl
from jax.experimental.pallas import tpu as pltpu

F32 = jnp.float32
BF16 = jnp.bfloat16

SUBLANES = 8
LANES = 128
VMEM_BUDGET_BYTES = 56 * 1024 * 1024

LRU_BLOCKS = 8
LRU_CONV = 4
LRU_C = 8.0
FFN_CONV = 3
HEAD = 64
PAIR = 2 * HEAD
CHUNK = 64
RMS_EPS = 1e-6
GN_EPS = 64e-5
TILE_ROWS = 512
FFN_COL_CHUNK = 512


def _dot(a, b):
    return jnp.dot(a, b, preferred_element_type=F32)


def _dot_nt(a, b):
    return lax.dot_general(a, b, (((1,), (1,)), ((), ())), preferred_element_type=F32)


def _dot_tn(a, b):
    return lax.dot_general(a, b, (((0,), (0,)), ((), ())), preferred_element_type=F32)


def _split(x):
    hi = x.astype(BF16)
    lo = (x - hi.astype(F32)).astype(BF16)
    return hi, lo


def _rms(x, g):
    return x * lax.rsqrt(jnp.mean(x * x, axis=-1, keepdims=True) + RMS_EPS) * g


def _softplus(x):
    return jnp.maximum(x, 0.0) + jnp.log1p(jnp.exp(-jnp.abs(x)))


def _shift_down(x, k, tail):
    rolled = pltpu.roll(x, k, 0)
    head = pltpu.roll(tail, k, 0)
    row = lax.broadcasted_iota(jnp.int32, head.shape, 0)
    first = jnp.where(row < k, head, rolled[:SUBLANES])
    return jnp.concatenate([first, rolled[SUBLANES:]], axis=0)


def _causal_conv(xs, tail, w, b):
    k = w.shape[0]
    y = b + xs * w[k - 1:k]
    for j in range(k - 1):
        y = y + _shift_down(xs, k - 1 - j, tail) * w[j:j + 1]
    return y


def _lru_kernel(x_ref, tail_in, h_in, g_ref, wx_ref, bx_ref, wy_ref, by_ref, cw_ref, cb_ref,
                wg_ref, bga_ref, bgx_ref, lam_ref, wo_ref, bo_ref,
                o_ref, tail_out, h_out, a_sc, b_sc, *, bb, tt):
    @pl.when(pl.program_id(1) == 0)
    def _():
        tail_out[...] = tail_in[...]
        h_out[...] = h_in[...]

    c = x_ref.shape[-1]
    m = bb * tt
    x = x_ref[...].reshape(m, c)
    hb = _rms(x, g_ref[...]).astype(BF16)
    gate = jax.nn.gelu(_dot(hb, wy_ref[...]) + by_ref[...])
    xb = _dot(hb, wx_ref[...]) + bx_ref[...]

    us = []
    for s in range(bb):
        xs = xb[s * tt:(s + 1) * tt]
        us.append(_causal_conv(xs, tail_out[s], cw_ref[...], cb_ref[...]))
        tail_out[s] = xs[tt - SUBLANES:]
    u = us[0] if bb == 1 else jnp.concatenate(us, axis=0)

    ub = u.astype(BF16)
    sp = _softplus(-lam_ref[...])
    bw = c // LRU_BLOCKS
    for n in range(LRU_BLOCKS):
        sl = slice(n * bw, (n + 1) * bw)
        gg = _dot(ub[:, sl], wg_ref[n])
        r = jax.nn.sigmoid(gg[:, :bw] + bga_ref[:, sl])
        i = jax.nn.sigmoid(gg[:, bw:] + bgx_ref[:, sl])
        log_a = -LRU_C * r * sp[:, sl]
        a = jnp.exp(log_a)
        a_sc[:, sl] = a
        b_sc[:, sl] = jnp.sqrt(-jnp.tanh(log_a) * (a * a + 1.0)) * (i * u[:, sl])

    row = lax.broadcasted_iota(jnp.int32, (SUBLANES, c), 0)
    for s in range(bb):
        def group(gi, carry, s=s):
            r0 = pl.multiple_of(s * tt + gi * SUBLANES, SUBLANES)
            av = a_sc[pl.ds(r0, SUBLANES), :]
            bv = b_sc[pl.ds(r0, SUBLANES), :]
            for sh in (1, 2, 4):
                a_sh = jnp.where(row >= sh, pltpu.roll(av, sh, 0), 1.0)
                b_sh = jnp.where(row >= sh, pltpu.roll(bv, sh, 0), 0.0)
                bv = av * b_sh + bv
                av = av * a_sh
            hv = bv + av * carry
            b_sc[pl.ds(r0, SUBLANES), :] = hv
            return hv[SUBLANES - 1:]

        h_out[s] = lax.fori_loop(0, tt // SUBLANES, group, h_out[s])

    y = _dot((b_sc[...] * gate).astype(BF16), wo_ref[...]) + bo_ref[...]
    o_ref[...] = (x + y).reshape(bb, tt, c)


def _ffn_kernel(x_ref, tail_in, g_ref, wup_ref, cw_ref, cb_ref, wdn_ref, gf_ref,
                o_ref, tail_out, *, bb, tt, final_norm):
    @pl.when(pl.program_id(1) == 0)
    def _():
        tail_out[...] = tail_in[...]

    c = x_ref.shape[-1]
    m = bb * tt
    dff = wdn_ref.shape[0]
    x = x_ref[...].reshape(m, c)
    hb = _rms(x, g_ref[...]).astype(BF16)

    acc = jnp.zeros((m, c), F32)
    for j in range(dff // FFN_COL_CHUNK):
        halves = []
        for half in range(2):
            cols = slice(half * dff + j * FFN_COL_CHUNK, half * dff + (j + 1) * FFN_COL_CHUNK)
            up = _dot(hb, wup_ref[:, cols])
            parts = []
            for s in range(bb):
                xs = up[s * tt:(s + 1) * tt]
                parts.append(_causal_conv(xs, tail_out[s, :, cols], cw_ref[:, cols], cb_ref[:, cols]))
                tail_out[s, :, cols] = xs[tt - SUBLANES:]
            halves.append(parts[0] if bb == 1 else jnp.concatenate(parts, axis=0))
        act = jax.nn.gelu(halves[0]) * halves[1]
        acc = acc + _dot(act.astype(BF16), wdn_ref[j * FFN_COL_CHUNK:(j + 1) * FFN_COL_CHUNK, :])

    out = x + acc
    if final_norm:
        out = _rms(out, gf_ref[...])
    o_ref[...] = out.reshape(bb, tt, c)


def _blockdiag(z):
    lane = lax.broadcasted_iota(jnp.int32, z.shape, 1)
    lo = jnp.where(lane < HEAD, z, 0.0).astype(BF16)
    hi = jnp.where(lane >= HEAD, z, 0.0).astype(BF16)
    return jnp.concatenate([lo, hi], axis=0)


def _seg_sum(z, ones_bd):
    hi, lo = _split(z)
    parts = []
    for p in range(z.shape[1] // PAIR):
        sl = slice(p * PAIR, (p + 1) * PAIR)
        parts.append(_dot(hi[:, sl], ones_bd) + _dot(lo[:, sl], ones_bd))
    return jnp.concatenate(parts, axis=1)


def _rwkv_kernel(x_ref, tail_in, z_in, g_ref, mu_ref, wr_ref, wk_ref, wv_ref, wo_ref,
                 w0_ref, w1_ref, w2_ref, a0_ref, a1_ref, a2_ref, g1_ref, g2_ref,
                 kk_ref, ka_ref, rk_ref, lnw_ref, lnb_ref,
                 o_ref, tail_out, z_out,
                 r_sc, lw_sc, k_sc, v_sc, a_sc, b_sc, y_sc, *, bb, tt):
    @pl.when(pl.program_id(1) == 0)
    def _():
        tail_out[...] = tail_in[...]
        z_out[...] = z_in[...]

    c = x_ref.shape[-1]
    m = bb * tt
    npairs = c // PAIR
    x = x_ref[...].reshape(m, c)
    hn = _rms(x, g_ref[...])

    prevs = []
    for s in range(bb):
        hs = hn[s * tt:(s + 1) * tt]
        prevs.append(_shift_down(hs, 1, tail_out[s]))
        tail_out[s] = hs[tt - SUBLANES:]
    prev = prevs[0] if bb == 1 else jnp.concatenate(prevs, axis=0)
    xx = prev - hn
    mix = lambda j: (hn + xx * mu_ref[j:j + 1, :]).astype(BF16)

    r = _dot(mix(0), wr_ref[...])
    k = _dot(mix(2), wk_ref[...])
    v = _dot(mix(3), wv_ref[...])
    wl = w0_ref[...] + _dot(jnp.tanh(_dot(mix(1), w1_ref[...])).astype(BF16), w2_ref[...])
    lw_sc[...] = -jnp.exp(-_softplus(-wl) - 0.5)
    asig = jax.nn.sigmoid(a0_ref[...] + _dot(_dot(mix(4), a1_ref[...]).astype(BF16), a2_ref[...]))
    gate = _dot(jax.nn.sigmoid(_dot(mix(5), g1_ref[...])).astype(BF16), g2_ref[...])

    prow = lax.broadcasted_iota(jnp.int32, (PAIR, PAIR), 0)
    pcol = lax.broadcasted_iota(jnp.int32, (PAIR, PAIR), 1)
    same_head = (prow < HEAD) == (pcol < HEAD)
    ones_bd = jnp.where(same_head, 1.0, 0.0).astype(BF16)

    kkr = k * kk_ref[...]
    kk = kkr / jnp.maximum(jnp.sqrt(_seg_sum(kkr * kkr, ones_bd)), 1e-12)
    k2 = k * (1.0 + (asig - 1.0) * ka_ref[...])
    r_sc[...] = r
    k_sc[...] = k2
    v_sc[...] = v
    a_sc[...] = -kk
    b_sc[...] = kk * asig

    trow = lax.broadcasted_iota(jnp.int32, (CHUNK, PAIR), 0)
    tcol = lax.broadcasted_iota(jnp.int32, (CHUNK, PAIR), 1) % HEAD
    strict = trow > tcol
    incl = trow >= tcol
    eye_pk = jnp.where(trow == tcol, 1.0, 0.0)
    tri = jnp.where(lax.broadcasted_iota(jnp.int32, (CHUNK, CHUNK), 0)
                    >= lax.broadcasted_iota(jnp.int32, (CHUNK, CHUNK), 1), 1.0, 0.0).astype(BF16)
    ones_sq = jnp.ones((PAIR, PAIR), BF16)
    chunks_per_slab = tt // CHUNK

    def chunk(q, carry):
        slab = q // chunks_per_slab
        rows = pl.ds(pl.multiple_of(q * CHUNK, CHUNK), CHUNK)
        lw = lw_sc[rows, :]
        lw_hi, lw_lo = _split(lw)
        lp = _dot(tri, lw_hi) + _dot(tri, lw_lo)
        p_incl = jnp.exp(lp)
        p_inv = jnp.exp(-lp)
        rh = (r_sc[rows, :] * p_incl).astype(BF16)
        ah = a_sc[rows, :] * jnp.exp(lp - lw)
        bh = b_sc[rows, :] * p_inv
        kh = k_sc[rows, :] * p_inv
        vv = v_sc[rows, :]
        lw_st = jnp.concatenate([lw_hi, lw_lo], axis=0)
        for p in range(npairs):
            sl = slice(p * PAIR, (p + 1) * PAIR)
            ah_p, bh_p, kh_p, v_p = ah[:, sl], bh[:, sl], kh[:, sl], vv[:, sl]
            ahb, bhb, khb, vb = (ah_p.astype(BF16), bh_p.astype(BF16), kh_p.astype(BF16),
                                 v_p.astype(BF16))
            lhs = jnp.concatenate([ahb, rh[:, sl]], axis=0)
            sc = _dot_nt(lhs, jnp.concatenate([_blockdiag(bh_p), _blockdiag(kh_p)], axis=0))
            a_ab = jnp.where(strict, sc[:CHUNK, :PAIR], 0.0)
            a_ak = jnp.where(strict, sc[:CHUNK, PAIR:], 0.0)
            a_rb = jnp.where(incl, sc[CHUNK:, :PAIR], 0.0)
            a_rk = jnp.where(incl, sc[CHUNK:, PAIR:], 0.0)
            tm = eye_pk + a_ab
            ak = a_ab
            for _ in range(5):
                ak = _dot(ak.astype(BF16), _blockdiag(ak))
                tm = tm + _dot(tm.astype(BF16), _blockdiag(ak))
            bd_v = _blockdiag(v_p)
            w1 = _dot(a_ak.astype(BF16), bd_v)
            tw = _dot(tm.astype(BF16), jnp.concatenate([_blockdiag(ah_p), _blockdiag(w1)], axis=1))
            z = z_out[slab, p]
            zr = _dot(jnp.concatenate([tw[:, :PAIR].astype(BF16), rh[:, sl]], axis=0), z.astype(BF16))
            u = zr[:CHUNK] + tw[:, PAIR:]
            ub = u.astype(BF16)
            y = zr[CHUNK:] + _dot(jnp.concatenate([a_rb, a_rk], axis=1).astype(BF16),
                                  jnp.concatenate([_blockdiag(u), bd_v], axis=0))
            y_sc[rows, sl] = y
            upd = _dot_tn(jnp.concatenate([bhb, khb], axis=0), jnp.concatenate([ub, vb], axis=0))
            col_decay = jnp.exp(_dot_tn(lw_st[:, sl], ones_sq))
            z_out[slab, p] = col_decay * (z + jnp.where(same_head, upd, 0.0))
        return carry

    lax.fori_loop(0, m // CHUNK, chunk, 0)

    y = y_sc[...]
    d = y - _seg_sum(y, ones_bd) * (1.0 / HEAD)
    var = _seg_sum(d * d, ones_bd) * (1.0 / HEAD)
    yn = d * lax.rsqrt(var + GN_EPS) * lnw_ref[...] + lnb_ref[...]
    bonus = _seg_sum(r_sc[...] * k_sc[...] * rk_ref[...], ones_bd) * v_sc[...]
    out = _dot(((yn + bonus) * gate).astype(BF16), wo_ref[...])
    o_ref[...] = (x + out).reshape(bb, tt, c)


def _tile(batch, seq, rows=TILE_ROWS):
    tt = min(seq, rows)
    bb = max(1, min(batch, rows // tt))
    assert seq % tt == 0 and batch % bb == 0 and tt % CHUNK == 0
    return bb, tt


def _const_spec(arr):
    nd = arr.ndim
    return pl.BlockSpec(arr.shape, lambda b, t, _nd=nd: (0,) * _nd, pipeline_mode=pl.Buffered(1))


def _row_spec(bb, rows, width):
    return pl.BlockSpec((bb, rows, width), lambda b, t: (b, 0, 0))


def _params(vmem_bytes):
    return pltpu.CompilerParams(dimension_semantics=("arbitrary", "arbitrary"),
                                vmem_limit_bytes=min(int(vmem_bytes), VMEM_BUDGET_BYTES))


def _nbytes(*arrs):
    return sum(a.size * a.dtype.itemsize for a in arrs)


def _lru_layer(x, tail, h0, w):
    b, t, c = x.shape
    bb, tt = _tile(b, t)
    m = bb * tt
    x_spec = pl.BlockSpec((bb, tt, c), lambda i, j: (i, j, 0))
    consts = [w[n] for n in ("g", "wx", "bx", "wy", "by", "cw", "cb", "wg", "bga", "bgx", "lam", "wo", "bo")]
    vmem = _nbytes(*consts) + 4 * m * c * 4 + 12 * m * c * 4
    return pl.pallas_call(
        functools.partial(_lru_kernel, bb=bb, tt=tt),
        grid=(b // bb, t // tt),
        in_specs=[x_spec, _row_spec(bb, SUBLANES, c), _row_spec(bb, 1, c)] + [_const_spec(a) for a in consts],
        out_specs=[x_spec, _row_spec(bb, SUBLANES, c), _row_spec(bb, 1, c)],
        out_shape=[jax.ShapeDtypeStruct(x.shape, F32), jax.ShapeDtypeStruct((b, SUBLANES, c), F32),
                   jax.ShapeDtypeStruct((b, 1, c), F32)],
        scratch_shapes=[pltpu.VMEM((m, c), F32), pltpu.VMEM((m, c), F32)],
        compiler_params=_params(vmem),
        name="lru_layer",
    )(x, tail, h0, *consts)


def _ffn_layer(x, tail, w, final_norm):
    b, t, c = x.shape
    bb, tt = _tile(b, t)
    m = bb * tt
    dff2 = w["wup"].shape[1]
    x_spec = pl.BlockSpec((bb, tt, c), lambda i, j: (i, j, 0))
    consts = [w[n] for n in ("g", "wup", "cw", "cb", "wdn", "gf")]
    vmem = _nbytes(*consts) + 4 * m * c * 4 + 4 * bb * SUBLANES * dff2 * 4 + 8 * m * c * 4
    return pl.pallas_call(
        functools.partial(_ffn_kernel, bb=bb, tt=tt, final_norm=final_norm),
        grid=(b // bb, t // tt),
        in_specs=[x_spec, _row_spec(bb, SUBLANES, dff2)] + [_const_spec(a) for a in consts],
        out_specs=[x_spec, _row_spec(bb, SUBLANES, dff2)],
        out_shape=[jax.ShapeDtypeStruct(x.shape, F32), jax.ShapeDtypeStruct((b, SUBLANES, dff2), F32)],
        compiler_params=_params(vmem),
        name="ffn_final" if final_norm else "ffn_layer",
    )(x, tail, *consts)


def _rwkv_layer(x, tail, z0, w):
    b, t, c = x.shape
    bb, tt = _tile(b, t, TILE_ROWS if t >= TILE_ROWS else TILE_ROWS // 2)
    m = bb * tt
    npairs = c // PAIR
    x_spec = pl.BlockSpec((bb, tt, c), lambda i, j: (i, j, 0))
    z_spec = pl.BlockSpec((bb, npairs, PAIR, PAIR), lambda i, j: (i, 0, 0, 0))
    consts = [w[n] for n in ("g", "mu", "wr", "wk", "wv", "wo", "w0", "w1", "w2", "a0", "a1", "a2",
                             "g1", "g2", "kk", "ka", "rk", "lnw", "lnb")]
    vmem = _nbytes(*consts) + 4 * m * c * 4 + 7 * m * c * 4 + 12 * m * c * 4
    return pl.pallas_call(
        functools.partial(_rwkv_kernel, bb=bb, tt=tt),
        grid=(b // bb, t // tt),
        in_specs=[x_spec, _row_spec(bb, SUBLANES, c), z_spec] + [_const_spec(a) for a in consts],
        out_specs=[x_spec, _row_spec(bb, SUBLANES, c), z_spec],
        out_shape=[jax.ShapeDtypeStruct(x.shape, F32), jax.ShapeDtypeStruct((b, SUBLANES, c), F32),
                   jax.ShapeDtypeStruct((b, npairs, PAIR, PAIR), F32)],
        scratch_shapes=[pltpu.VMEM((m, c), F32) for _ in range(7)],
        compiler_params=_params(vmem),
        name="rwkv_layer",
    )(x, tail, z0, *consts)


def _to_tail(buf):
    return jnp.pad(buf, ((0, 0), (SUBLANES - buf.shape[1], 0), (0, 0)))


def _wkv_to_pairs(s):
    b, h, n, _ = s.shape
    st = jnp.swapaxes(s, -1, -2).reshape(b, h // 2, 2, n, n)
    z = jnp.einsum("bphji,hg->bphjgi", st, jnp.eye(2, dtype=s.dtype))
    return z.reshape(b, h // 2, 2 * n, 2 * n)


def _pairs_to_wkv(z):
    b, p, n2, _ = z.shape
    n = n2 // 2
    z = z.reshape(b, p, 2, n, 2, n)
    st = jnp.stack([z[:, :, 0, :, 0, :], z[:, :, 1, :, 1, :]], axis=2)
    return jnp.swapaxes(st, -1, -2).reshape(b, 2 * p, n, n)


def _pad_cols(w, width):
    return jnp.pad(w, ((0, 0), (0, width - w.shape[1])))


def _pad_rows(w, height):
    return jnp.pad(w, ((0, height - w.shape[0]), (0, 0)))


def kernel(x_prompt, x_sample, state_lru_conv, state_lru_h, state_rwkv_shift, state_rwkv_wkv, state_ffn_conv,
           norm_mix, norm_ffn, norm_final,
           lru_w_x, lru_b_x, lru_w_y, lru_b_y, lru_conv_w, lru_conv_b, lru_ga_w, lru_ga_b, lru_gx_w, lru_gx_b,
           lru_lambda, lru_w_out, lru_b_out,
           rw_mu, rw_w_r, rw_w_k, rw_w_v, rw_w_o, rw_w0, rw_w1, rw_w2, rw_a0, rw_a1, rw_a2, rw_g1, rw_g2,
           rw_k_k, rw_k_a, rw_r_k, rw_ln_w, rw_ln_b,
           ffn_w_up, ffn_conv_w, ffn_conv_b, ffn_w_down):
    c = x_prompt.shape[-1]
    row = lambda v: v.reshape(1, -1).astype(F32)
    bf = lambda v: v.astype(BF16)

    lru_w = dict(
        g=row(norm_mix[0]), wx=bf(lru_w_x[0]), bx=row(lru_b_x[0]), wy=bf(lru_w_y[0]), by=row(lru_b_y[0]),
        cw=lru_conv_w[0], cb=row(lru_conv_b[0]),
        wg=bf(jnp.concatenate([lru_ga_w[0], lru_gx_w[0]], axis=-1)),
        bga=row(lru_ga_b[0]), bgx=row(lru_gx_b[0]), lam=row(lru_lambda[0]),
        wo=bf(lru_w_out[0]), bo=row(lru_b_out[0]))
    ffn_w = [dict(g=row(norm_ffn[l]), wup=bf(ffn_w_up[l]), cw=ffn_conv_w[l], cb=row(ffn_conv_b[l]),
                  wdn=bf(ffn_w_down[l]), gf=row(norm_final)) for l in range(2)]
    rw_w = dict(
        g=row(norm_mix[1]), mu=rw_mu[0], wr=bf(rw_w_r[0]), wk=bf(rw_w_k[0]), wv=bf(rw_w_v[0]), wo=bf(rw_w_o[0]),
        w0=row(rw_w0[0]), w1=bf(_pad_cols(rw_w1[0], LANES)), w2=bf(_pad_rows(rw_w2[0], LANES)),
        a0=row(rw_a0[0]), a1=bf(_pad_cols(rw_a1[0], LANES)), a2=bf(_pad_rows(rw_a2[0], LANES)),
        g1=bf(rw_g1[0]), g2=bf(rw_g2[0]),
        kk=row(rw_k_k[0]), ka=row(rw_k_a[0]), rk=row(rw_r_k[0]), lnw=row(rw_ln_w[0]), lnb=row(rw_ln_b[0]))

    def trunk(x, lru_conv, lru_h, rw_shift, rw_wkv, ffn_conv):
        x, lc, lh = _lru_layer(x, _to_tail(lru_conv[0]), lru_h[0][:, None, :], lru_w)
        x, fc0 = _ffn_layer(x, _to_tail(ffn_conv[0]), ffn_w[0], final_norm=False)
        x, rs, zz = _rwkv_layer(x, _to_tail(rw_shift[0][:, None, :]), _wkv_to_pairs(rw_wkv[0]), rw_w)
        y, fc1 = _ffn_layer(x, _to_tail(ffn_conv[1]), ffn_w[1], final_norm=True)
        return (y, lc[None, :, SUBLANES - (LRU_CONV - 1):], lh[None, :, 0], rs[None, :, SUBLANES - 1],
                _pairs_to_wkv(zz)[None],
                jnp.stack([fc0[:, SUBLANES - (FFN_CONV - 1):], fc1[:, SUBLANES - (FFN_CONV - 1):]]))

    bp = x_prompt.shape[0]
    zeros = lambda *shape: jnp.zeros(shape, F32)
    heads = state_rwkv_wkv.shape[2]
    prompt = trunk(x_prompt, zeros(1, bp, LRU_CONV - 1, c), zeros(1, bp, c), zeros(1, bp, c),
                   zeros(1, bp, heads, HEAD, HEAD), zeros(2, bp, FFN_CONV - 1, ffn_w_up.shape[-1]))
    sample = trunk(x_sample, state_lru_conv, state_lru_h, state_rwkv_shift, state_rwkv_wkv, state_ffn_conv)
    return (prompt[0], sample[0]) + prompt[1:] + sample[1:]
```

```python
import functools

import jax
import jax.numpy as jnp
from jax import lax
from jax.experimental import pallas as pl
from jax.experimental.pallas import tpu as pltpu

F32 = jnp.float32
BF16 = jnp.bfloat16

SUBLANES = 8
LANES = 128
VMEM_BUDGET_BYTES = 56 * 1024 * 1024

LRU_BLOCKS = 8
LRU_CONV = 4
LRU_C = 8.0
FFN_CONV = 3
HEAD = 64
PAIR = 2 * HEAD
CHUNK = 64
TOT_ROWS = 16
RMS_EPS = 1e-6
GN_EPS = 64e-5
TILE_ROWS = 512
FFN_COL_CHUNK = 1536


def _dot(a, b):
    return jnp.dot(a, b, preferred_element_type=F32)


def _dot_nt(a, b):
    return lax.dot_general(a, b, (((1,), (1,)), ((), ())), preferred_element_type=F32)


def _dot_tn(a, b):
    return lax.dot_general(a, b, (((0,), (0,)), ((), ())), preferred_element_type=F32)


def _split(x):
    hi = x.astype(BF16)
    lo = (x - hi.astype(F32)).astype(BF16)
    return hi, lo


def _rms(x, g):
    return x * lax.rsqrt(jnp.mean(x * x, axis=-1, keepdims=True) + RMS_EPS) * g


def _softplus(x):
    return jnp.maximum(x, 0.0) + jnp.log1p(jnp.exp(-jnp.abs(x)))


def _shift_down(x, k, tail):
    rolled = pltpu.roll(x, k, 0)
    head = pltpu.roll(tail, k, 0)
    row = lax.broadcasted_iota(jnp.int32, head.shape, 0)
    first = jnp.where(row < k, head, rolled[:SUBLANES])
    return jnp.concatenate([first, rolled[SUBLANES:]], axis=0)


def _causal_conv(xs, tail, w, b):
    k = w.shape[0]
    y = b + xs * w[k - 1:k]
    for j in range(k - 1):
        y = y + _shift_down(xs, k - 1 - j, tail) * w[j:j + 1]
    return y


def _lru_kernel(x_ref, tail_in, h_in, g_ref, wx_ref, bx_ref, wy_ref, by_ref, cw_ref, cb_ref,
                wg_ref, bga_ref, bgx_ref, lam_ref, wo_ref, bo_ref,
                o_ref, tail_out, h_out, a_sc, b_sc, *, bb, tt):
    @pl.when(pl.program_id(1) == 0)
    def _():
        tail_out[...] = tail_in[...]
        h_out[...] = h_in[...]

    c = x_ref.shape[-1]
    m = bb * tt
    x = x_ref[...].reshape(m, c)
    hb = _rms(x, g_ref[...]).astype(BF16)
    gate = jax.nn.gelu(_dot(hb, wy_ref[...]) + by_ref[...])
    xb = _dot(hb, wx_ref[...]) + bx_ref[...]

    us = []
    for s in range(bb):
        xs = xb[s * tt:(s + 1) * tt]
        us.append(_causal_conv(xs, tail_out[s], cw_ref[...], cb_ref[...]))
        tail_out[s] = xs[tt - SUBLANES:]
    u = us[0] if bb == 1 else jnp.concatenate(us, axis=0)

    ub = u.astype(BF16)
    sp = _softplus(-lam_ref[...])
    bw = c // LRU_BLOCKS
    for n in range(LRU_BLOCKS):
        sl = slice(n * bw, (n + 1) * bw)
        gg = _dot(ub[:, sl], wg_ref[n])
        r = jax.nn.sigmoid(gg[:, :bw] + bga_ref[:, sl])
        i = jax.nn.sigmoid(gg[:, bw:] + bgx_ref[:, sl])
        log_a = -LRU_C * r * sp[:, sl]
        a = jnp.exp(log_a)
        a_sc[:, sl] = a
        b_sc[:, sl] = jnp.sqrt(-jnp.tanh(log_a) * (a * a + 1.0)) * (i * u[:, sl])

    row = lax.broadcasted_iota(jnp.int32, (SUBLANES, c), 0)
    for s in range(bb):
        def group(gi, carry, s=s):
            r0 = pl.multiple_of(s * tt + gi * SUBLANES, SUBLANES)
            av = a_sc[pl.ds(r0, SUBLANES), :]
            bv = b_sc[pl.ds(r0, SUBLANES), :]
            for sh in (1, 2, 4):
                a_sh = jnp.where(row >= sh, pltpu.roll(av, sh, 0), 1.0)
                b_sh = jnp.where(row >= sh, pltpu.roll(bv, sh, 0), 0.0)
                bv = av * b_sh + bv
                av = av * a_sh
            hv = bv + av * carry
            b_sc[pl.ds(r0, SUBLANES), :] = hv
            return hv[SUBLANES - 1:]

        h_out[s] = lax.fori_loop(0, tt // SUBLANES, group, h_out[s])

    y = _dot((b_sc[...] * gate).astype(BF16), wo_ref[...]) + bo_ref[...]
    o_ref[...] = (x + y).reshape(bb, tt, c)


def _ffn_kernel(x_ref, tail_in, g_ref, wup_ref, cw_ref, cb_ref, wdn_ref, gf_ref,
                o_ref, tail_out, *, bb, tt, final_norm):
    @pl.when(pl.program_id(1) == 0)
    def _():
        tail_out[...] = tail_in[...]

    c = x_ref.shape[-1]
    m = bb * tt
    dff = wdn_ref.shape[0]
    x = x_ref[...].reshape(m, c)
    hb = _rms(x, g_ref[...]).astype(BF16)

    acc = jnp.zeros((m, c), F32)
    for j in range(dff // FFN_COL_CHUNK):
        halves = []
        for half in range(2):
            cols = slice(half * dff + j * FFN_COL_CHUNK, half * dff + (j + 1) * FFN_COL_CHUNK)
            up = _dot(hb, wup_ref[:, cols])
            parts = []
            for s in range(bb):
                xs = up[s * tt:(s + 1) * tt]
                parts.append(_causal_conv(xs, tail_out[s, :, cols], cw_ref[:, cols], cb_ref[:, cols]))
                tail_out[s, :, cols] = xs[tt - SUBLANES:]
            halves.append(parts[0] if bb == 1 else jnp.concatenate(parts, axis=0))
        act = jax.nn.gelu(halves[0]) * halves[1]
        acc = acc + _dot(act.astype(BF16), wdn_ref[j * FFN_COL_CHUNK:(j + 1) * FFN_COL_CHUNK, :])

    out = x + acc
    if final_norm:
        out = _rms(out, gf_ref[...])
    o_ref[...] = out.reshape(bb, tt, c)


def _blockdiag(z, head_masks):
    zb = z.astype(BF16)
    return jnp.concatenate([zb * head_masks[0], zb * head_masks[1]], axis=0)


def _seg_sum(z, ones_seg):
    hi, lo = _split(z)
    width = ones_seg.shape[0]
    parts = []
    for p in range(z.shape[1] // width):
        sl = slice(p * width, (p + 1) * width)
        parts.append(_dot(hi[:, sl], ones_seg) + _dot(lo[:, sl], ones_seg))
    return jnp.concatenate(parts, axis=1)


def _rwkv_kernel(x_ref, tail_in, z_in, g_ref, mu_ref, wr_ref, wk_ref, wv_ref, wo_ref,
                 w0_ref, w1_ref, w2_ref, a0_ref, a1_ref, a2_ref, g1_ref, g2_ref,
                 kk_ref, ka_ref, rk_ref, lnw_ref, lnb_ref,
                 o_ref, tail_out, z_out,
                 bonus_sc, y_sc, rh_sc, ah_sc, bh_sc, kh_sc, vb_sc, tot_sc, *, bb, tt):
    @pl.when(pl.program_id(1) == 0)
    def _():
        tail_out[...] = tail_in[...]
        z_out[...] = z_in[...]

    c = x_ref.shape[-1]
    m = bb * tt
    npairs = c // PAIR
    x = x_ref[...].reshape(m, c)
    hn = _rms(x, g_ref[...])

    prevs = []
    for s in range(bb):
        hs = hn[s * tt:(s + 1) * tt]
        prevs.append(_shift_down(hs, 1, tail_out[s]))
        tail_out[s] = hs[tt - SUBLANES:]
    prev = prevs[0] if bb == 1 else jnp.concatenate(prevs, axis=0)
    xx = prev - hn
    mix = lambda j: (hn + xx * mu_ref[j:j + 1, :]).astype(BF16)

    r = _dot(mix(0), wr_ref[...])
    k = _dot(mix(2), wk_ref[...])
    v = _dot(mix(3), wv_ref[...])
    wl = w0_ref[...] + _dot(jnp.tanh(_dot(mix(1), w1_ref[...])).astype(BF16), w2_ref[...])
    lw = -jnp.exp(-_softplus(-wl) - 0.5)
    asig = jax.nn.sigmoid(a0_ref[...] + _dot(_dot(mix(4), a1_ref[...]).astype(BF16), a2_ref[...]))
    gate = _dot(jax.nn.sigmoid(_dot(mix(5), g1_ref[...])).astype(BF16), g2_ref[...])

    srow = lax.broadcasted_iota(jnp.int32, (2 * PAIR, 2 * PAIR), 0)
    scol = lax.broadcasted_iota(jnp.int32, (2 * PAIR, 2 * PAIR), 1)
    ones_seg = jnp.where(srow // HEAD == scol // HEAD, 1.0, 0.0).astype(BF16)

    kkr = k * kk_ref[...]
    kk = kkr / jnp.maximum(jnp.sqrt(_seg_sum(kkr * kkr, ones_seg)), 1e-12)
    k2 = k * (1.0 + (asig - 1.0) * ka_ref[...])

    tri = jnp.where(lax.broadcasted_iota(jnp.int32, (CHUNK, CHUNK), 0)
                    >= lax.broadcasted_iota(jnp.int32, (CHUNK, CHUNK), 1), 1.0, 0.0).astype(BF16)
    lw_hi, lw_lo = _split(lw)
    lp = jnp.concatenate([_dot(tri, lw_hi[i * CHUNK:(i + 1) * CHUNK]) + _dot(tri, lw_lo[i * CHUNK:(i + 1) * CHUNK])
                          for i in range(m // CHUNK)], axis=0)
    p_inv = jnp.exp(-lp)
    bonus_sc[...] = _seg_sum(r * k2 * rk_ref[...], ones_seg) * v
    rh_sc[...] = (r * jnp.exp(lp)).astype(BF16)
    ah_sc[...] = (-kk * jnp.exp(lp - lw)).astype(BF16)
    bh_sc[...] = (kk * asig * p_inv).astype(BF16)
    kh_sc[...] = (k2 * p_inv).astype(BF16)
    vb_sc[...] = v.astype(BF16)
    tot_row = lax.broadcasted_iota(jnp.int32, (TOT_ROWS, c), 0)
    for i in range(m // CHUNK):
        tot = lp[(i + 1) * CHUNK - 1:(i + 1) * CHUNK]
        tot_hi = tot.astype(BF16).astype(F32)
        blk = jnp.where(tot_row == 0, tot_hi, jnp.where(tot_row == 1, tot - tot_hi, 0.0))
        tot_sc[i * TOT_ROWS:(i + 1) * TOT_ROWS, :] = blk.astype(BF16)

    trow = lax.broadcasted_iota(jnp.int32, (CHUNK, PAIR), 0)
    tcol = lax.broadcasted_iota(jnp.int32, (CHUNK, PAIR), 1) % HEAD
    strict = trow > tcol
    incl = trow >= tcol
    eye_pk = jnp.where(trow == tcol, 1.0, 0.0)
    prow = lax.broadcasted_iota(jnp.int32, (PAIR, PAIR), 0)
    pcol = lax.broadcasted_iota(jnp.int32, (PAIR, PAIR), 1)
    same_head = (prow < HEAD) == (pcol < HEAD)
    lane = lax.broadcasted_iota(jnp.int32, (1, PAIR), 1)
    head_masks = (jnp.where(lane < HEAD, 1.0, 0.0).astype(BF16), jnp.where(lane >= HEAD, 1.0, 0.0).astype(BF16))
    bd = lambda t: _blockdiag(t, head_masks)
    ones_tot = jnp.ones((TOT_ROWS, PAIR), BF16)
    cps = tt // CHUNK
    group = 2 if bb % 2 == 0 else 1

    def step(it, carry):
        slab0 = it * group if cps == 1 else lax.div(it, cps) * group
        ci = 0 if cps == 1 else lax.rem(it, cps)
        streams = []
        for g in range(group):
            rows = pl.ds(pl.multiple_of((slab0 + g) * tt + ci * CHUNK, CHUNK), CHUNK)
            streams += [(slab0 + g, p, rows, slice(p * PAIR, (p + 1) * PAIR)) for p in range(npairs)]
        ns = range(len(streams))
        ld = lambda ref: [ref[rows, sl] for (_, _, rows, sl) in streams]
        rh, ahb, bhb, khb, vb = ld(rh_sc), ld(ah_sc), ld(bh_sc), ld(kh_sc), ld(vb_sc)
        tot = [tot_sc[pl.ds(pl.multiple_of(((slab0 + g) * cps + ci) * TOT_ROWS, TOT_ROWS), TOT_ROWS), :]
               for g in range(group)]
        sc = [_dot_nt(jnp.concatenate([ahb[s], rh[s]], axis=0),
                      jnp.concatenate([bd(bhb[s]), bd(khb[s])], axis=0)) for s in ns]
        col_decay = [jnp.exp(_dot_tn(tot[s // npairs][:, streams[s][3]], ones_tot)) for s in ns]
        a_ab = [jnp.where(strict, t[:CHUNK, :PAIR], 0.0) for t in sc]
        a_ak = [jnp.where(strict, t[:CHUNK, PAIR:], 0.0) for t in sc]
        a_r = [jnp.concatenate([jnp.where(incl, t[CHUNK:, :PAIR], 0.0),
                                jnp.where(incl, t[CHUNK:, PAIR:], 0.0)], axis=1).astype(BF16) for t in sc]
        bd_v = [bd(t) for t in vb]
        ak = [_dot(t.astype(BF16), bd(t)) for t in a_ab]
        w1 = [_dot(a_ak[s].astype(BF16), bd_v[s]) for s in ns]
        tm = [eye_pk + t for t in a_ab]
        for n in range(5):
            bd_ak = [bd(t) for t in ak]
            tm = [tm[s] + _dot(tm[s].astype(BF16), bd_ak[s]) for s in ns]
            if n < 4:
                ak = [_dot(ak[s].astype(BF16), bd_ak[s]) for s in ns]
        tw = [_dot(tm[s].astype(BF16), jnp.concatenate([bd(ahb[s]), bd(w1[s])], axis=1)) for s in ns]
        z = [z_out[slab, p] for (slab, p, _, _) in streams]
        zr = [_dot(jnp.concatenate([tw[s][:, :PAIR].astype(BF16), rh[s]], axis=0), z[s].astype(BF16))
              for s in ns]
        u = [zr[s][:CHUNK] + tw[s][:, PAIR:] for s in ns]
        upd = [_dot_tn(jnp.concatenate([bhb[s], khb[s]], axis=0),
                       jnp.concatenate([u[s].astype(BF16), vb[s]], axis=0)) for s in ns]
        for s, (slab, p, _, _) in enumerate(streams):
            z_out[slab, p] = col_decay[s] * (z[s] + jnp.where(same_head, upd[s], 0.0))
        y = [zr[s][CHUNK:] + _dot(a_r[s], jnp.concatenate([bd(u[s]), bd_v[s]], axis=0)) for s in ns]
        for s, (_, _, rows, sl) in enumerate(streams):
            y_sc[rows, sl] = y[s]
        return carry

    lax.fori_loop(0, (bb // group) * cps, step, 0)

    y = y_sc[...]
    d = y - _seg_sum(y, ones_seg) * (1.0 / HEAD)
    var = _seg_sum(d * d, ones_seg) * (1.0 / HEAD)
    yn = d * lax.rsqrt(var + GN_EPS) * lnw_ref[...] + lnb_ref[...]
    out = _dot(((yn + bonus_sc[...]) * gate).astype(BF16), wo_ref[...])
    o_ref[...] = (x_ref[...].reshape(m, c) + out).reshape(bb, tt, c)


def _tile(batch, seq, rows=TILE_ROWS, max_tt=TILE_ROWS):
    tt = min(seq, max_tt)
    bb = max(1, min(batch, rows // tt))
    assert seq % tt == 0 and batch % bb == 0 and tt % CHUNK == 0
    return bb, tt


def _const_spec(arr):
    nd = arr.ndim
    return pl.BlockSpec(arr.shape, lambda b, t, _nd=nd: (0,) * _nd, pipeline_mode=pl.Buffered(1))


def _row_spec(bb, rows, width):
    return pl.BlockSpec((bb, rows, width), lambda b, t: (b, 0, 0))


def _params(vmem_bytes):
    return pltpu.CompilerParams(dimension_semantics=("arbitrary", "arbitrary"),
                                vmem_limit_bytes=min(int(vmem_bytes), VMEM_BUDGET_BYTES))


def _nbytes(*arrs):
    return sum(a.size * a.dtype.itemsize for a in arrs)


def _lru_layer(x, tail, h0, w):
    b, t, c = x.shape
    bb, tt = _tile(b, t)
    m = bb * tt
    x_spec = pl.BlockSpec((bb, tt, c), lambda i, j: (i, j, 0))
    consts = [w[n] for n in ("g", "wx", "bx", "wy", "by", "cw", "cb", "wg", "bga", "bgx", "lam", "wo", "bo")]
    vmem = _nbytes(*consts) + 4 * m * c * 4 + 12 * m * c * 4
    return pl.pallas_call(
        functools.partial(_lru_kernel, bb=bb, tt=tt),
        grid=(b // bb, t // tt),
        in_specs=[x_spec, _row_spec(bb, SUBLANES, c), _row_spec(bb, 1, c)] + [_const_spec(a) for a in consts],
        out_specs=[x_spec, _row_spec(bb, SUBLANES, c), _row_spec(bb, 1, c)],
        out_shape=[jax.ShapeDtypeStruct(x.shape, F32), jax.ShapeDtypeStruct((b, SUBLANES, c), F32),
                   jax.ShapeDtypeStruct((b, 1, c), F32)],
        scratch_shapes=[pltpu.VMEM((m, c), F32), pltpu.VMEM((m, c), F32)],
        compiler_params=_params(vmem),
        name="lru_layer",
    )(x, tail, h0, *consts)


def _ffn_layer(x, tail, w, final_norm):
    b, t, c = x.shape
    bb, tt = _tile(b, t)
    m = bb * tt
    dff2 = w["wup"].shape[1]
    x_spec = pl.BlockSpec((bb, tt, c), lambda i, j: (i, j, 0))
    consts = [w[n] for n in ("g", "wup", "cw", "cb", "wdn", "gf")]
    vmem = _nbytes(*consts) + 4 * m * c * 4 + 4 * bb * SUBLANES * dff2 * 4 + 8 * m * c * 4
    return pl.pallas_call(
        functools.partial(_ffn_kernel, bb=bb, tt=tt, final_norm=final_norm),
        grid=(b // bb, t // tt),
        in_specs=[x_spec, _row_spec(bb, SUBLANES, dff2)] + [_const_spec(a) for a in consts],
        out_specs=[x_spec, _row_spec(bb, SUBLANES, dff2)],
        out_shape=[jax.ShapeDtypeStruct(x.shape, F32), jax.ShapeDtypeStruct((b, SUBLANES, dff2), F32)],
        compiler_params=_params(vmem),
        name="ffn_final" if final_norm else "ffn_layer",
    )(x, tail, *consts)


def _rwkv_layer(x, tail, z0, w):
    b, t, c = x.shape
    bb, tt = _tile(b, t, TILE_ROWS if t >= TILE_ROWS else TILE_ROWS // 2, max_tt=TILE_ROWS // 2)
    m = bb * tt
    npairs = c // PAIR
    x_spec = pl.BlockSpec((bb, tt, c), lambda i, j: (i, j, 0))
    z_spec = pl.BlockSpec((bb, npairs, PAIR, PAIR), lambda i, j: (i, 0, 0, 0))
    consts = [w[n] for n in ("g", "mu", "wr", "wk", "wv", "wo", "w0", "w1", "w2", "a0", "a1", "a2",
                             "g1", "g2", "kk", "ka", "rk", "lnw", "lnb")]
    scratch = ([pltpu.VMEM((m, c), F32) for _ in range(2)] + [pltpu.VMEM((m, c), BF16) for _ in range(5)]
               + [pltpu.VMEM((m // CHUNK * TOT_ROWS, c), BF16)])
    vmem = _nbytes(*consts) + 4 * m * c * 4 + (2 * 4 + 5 * 2) * m * c + 16 * m * c * 4
    return pl.pallas_call(
        functools.partial(_rwkv_kernel, bb=bb, tt=tt),
        grid=(b // bb, t // tt),
        in_specs=[x_spec, _row_spec(bb, SUBLANES, c), z_spec] + [_const_spec(a) for a in consts],
        out_specs=[x_spec, _row_spec(bb, SUBLANES, c), z_spec],
        out_shape=[jax.ShapeDtypeStruct(x.shape, F32), jax.ShapeDtypeStruct((b, SUBLANES, c), F32),
                   jax.ShapeDtypeStruct((b, npairs, PAIR, PAIR), F32)],
        scratch_shapes=scratch,
        compiler_params=_params(vmem),
        name="rwkv_layer",
    )(x, tail, z0, *consts)


def _to_tail(buf):
    return jnp.pad(buf, ((0, 0), (SUBLANES - buf.shape[1], 0), (0, 0)))


def _wkv_to_pairs(s):
    b, h, n, _ = s.shape
    st = jnp.swapaxes(s, -1, -2).reshape(b, h // 2, 2, n, n)
    z = jnp.einsum("bphji,hg->bphjgi", st, jnp.eye(2, dtype=s.dtype))
    return z.reshape(b, h // 2, 2 * n, 2 * n)


def _pairs_to_wkv(z):
    b, p, n2, _ = z.shape
    n = n2 // 2
    z = z.reshape(b, p, 2, n, 2, n)
    st = jnp.stack([z[:, :, 0, :, 0, :], z[:, :, 1, :, 1, :]], axis=2)
    return jnp.swapaxes(st, -1, -2).reshape(b, 2 * p, n, n)


def _pad_cols(w, width):
    return jnp.pad(w, ((0, 0), (0, width - w.shape[1])))


def _pad_rows(w, height):
    return jnp.pad(w, ((0, height - w.shape[0]), (0, 0)))


def kernel(x_prompt, x_sample, state_lru_conv, state_lru_h, state_rwkv_shift, state_rwkv_wkv, state_ffn_conv,
           norm_mix, norm_ffn, norm_final,
           lru_w_x, lru_b_x, lru_w_y, lru_b_y, lru_conv_w, lru_conv_b, lru_ga_w, lru_ga_b, lru_gx_w, lru_gx_b,
           lru_lambda, lru_w_out, lru_b_out,
           rw_mu, rw_w_r, rw_w_k, rw_w_v, rw_w_o, rw_w0, rw_w1, rw_w2, rw_a0, rw_a1, rw_a2, rw_g1, rw_g2,
           rw_k_k, rw_k_a, rw_r_k, rw_ln_w, rw_ln_b,
           ffn_w_up, ffn_conv_w, ffn_conv_b, ffn_w_down):
    c = x_prompt.shape[-1]
    row = lambda v: v.reshape(1, -1).astype(F32)
    bf = lambda v: v.astype(BF16)

    lru_w = dict(
        g=row(norm_mix[0]), wx=bf(lru_w_x[0]), bx=row(lru_b_x[0]), wy=bf(lru_w_y[0]), by=row(lru_b_y[0]),
        cw=lru_conv_w[0], cb=row(lru_conv_b[0]),
        wg=bf(jnp.concatenate([lru_ga_w[0], lru_gx_w[0]], axis=-1)),
        bga=row(lru_ga_b[0]), bgx=row(lru_gx_b[0]), lam=row(lru_lambda[0]),
        wo=bf(lru_w_out[0]), bo=row(lru_b_out[0]))
    ffn_w = [dict(g=row(norm_ffn[l]), wup=bf(ffn_w_up[l]), cw=ffn_conv_w[l], cb=row(ffn_conv_b[l]),
                  wdn=bf(ffn_w_down[l]), gf=row(norm_final)) for l in range(2)]
    rw_w = dict(
        g=row(norm_mix[1]), mu=rw_mu[0], wr=bf(rw_w_r[0]), wk=bf(rw_w_k[0]), wv=bf(rw_w_v[0]), wo=bf(rw_w_o[0]),
        w0=row(rw_w0[0]), w1=bf(_pad_cols(rw_w1[0], LANES)), w2=bf(_pad_rows(rw_w2[0], LANES)),
        a0=row(rw_a0[0]), a1=bf(_pad_cols(rw_a1[0], LANES)), a2=bf(_pad_rows(rw_a2[0], LANES)),
        g1=bf(rw_g1[0]), g2=bf(rw_g2[0]),
        kk=row(rw_k_k[0]), ka=row(rw_k_a[0]), rk=row(rw_r_k[0]), lnw=row(rw_ln_w[0]), lnb=row(rw_ln_b[0]))

    def trunk(x, lru_conv, lru_h, rw_shift, rw_wkv, ffn_conv):
        x, lc, lh = _lru_layer(x, _to_tail(lru_conv[0]), lru_h[0][:, None, :], lru_w)
        x, fc0 = _ffn_layer(x, _to_tail(ffn_conv[0]), ffn_w[0], final_norm=False)
        x, rs, zz = _rwkv_layer(x, _to_tail(rw_shift[0][:, None, :]), _wkv_to_pairs(rw_wkv[0]), rw_w)
        y, fc1 = _ffn_layer(x, _to_tail(ffn_conv[1]), ffn_w[1], final_norm=True)
        return (y, lc[None, :, SUBLANES - (LRU_CONV - 1):], lh[None, :, 0], rs[None, :, SUBLANES - 1],
                _pairs_to_wkv(zz)[None],
                jnp.stack([fc0[:, SUBLANES - (FFN_CONV - 1):], fc1[:, SUBLANES - (FFN_CONV - 1):]]))

    bp = x_prompt.shape[0]
    zeros = lambda *shape: jnp.zeros(shape, F32)
    heads = state_rwkv_wkv.shape[2]
    prompt = trunk(x_prompt, zeros(1, bp, LRU_CONV - 1, c), zeros(1, bp, c), zeros(1, bp, c),
                   zeros(1, bp, heads, HEAD, HEAD), zeros(2, bp, FFN_CONV - 1, ffn_w_up.shape[-1]))
    sample = trunk(x_sample, state_lru_conv, state_lru_h, state_rwkv_shift, state_rwkv_wkv, state_ffn_conv)
    return (prompt[0], sample[0]) + prompt[1:] + sample[1:]
```

```python
import functools

import jax
import jax.numpy as jnp
from jax import lax
from jax.experimental import pallas as pl
from jax.experimental.pallas import tpu as pltpu

F32 = jnp.float32
BF16 = jnp.bfloat16

SUBLANES = 8
LANES = 128
VMEM_BUDGET_BYTES = 56 * 1024 * 1024

LRU_BLOCKS = 8
LRU_CONV = 4
LRU_C = 8.0
FFN_CONV = 3
HEAD = 64
PAIR = 2 * HEAD
CHUNK = 64
TOT_ROWS = 16
RMS_EPS = 1e-6
GN_EPS = 64e-5
TILE_ROWS = 512
FFN_COL_CHUNK = 1536


def _dot(a, b):
    return jnp.dot(a, b, preferred_element_type=F32)


def _dot_nt(a, b):
    return lax.dot_general(a, b, (((1,), (1,)), ((), ())), preferred_element_type=F32)


def _dot_tn(a, b):
    return lax.dot_general(a, b, (((0,), (0,)), ((), ())), preferred_element_type=F32)


def _split(x):
    hi = x.astype(BF16)
    lo = (x - hi.astype(F32)).astype(BF16)
    return hi, lo


def _rms(x, g):
    return x * lax.rsqrt(jnp.mean(x * x, axis=-1, keepdims=True) + RMS_EPS) * g


def _softplus(x):
    return jnp.maximum(x, 0.0) + jnp.log1p(jnp.exp(-jnp.abs(x)))


def _shift_down(x, k, tail):
    rolled = pltpu.roll(x, k, 0)
    head = pltpu.roll(tail, k, 0)
    row = lax.broadcasted_iota(jnp.int32, head.shape, 0)
    first = jnp.where(row < k, head, rolled[:SUBLANES])
    return jnp.concatenate([first, rolled[SUBLANES:]], axis=0)


def _causal_conv(xs, tail, w, b):
    k = w.shape[0]
    y = b + xs * w[k - 1:k]
    for j in range(k - 1):
        y = y + _shift_down(xs, k - 1 - j, tail) * w[j:j + 1]
    return y


def _lru_kernel(x_ref, tail_in, h_in, g_ref, wx_ref, bx_ref, wy_ref, by_ref, cw_ref, cb_ref,
                wg_ref, bga_ref, bgx_ref, lam_ref, wo_ref, bo_ref,
                o_ref, tail_out, h_out, a_sc, b_sc, *, bb, tt):
    @pl.when(pl.program_id(1) == 0)
    def _():
        tail_out[...] = tail_in[...]
        h_out[...] = h_in[...]

    c = x_ref.shape[-1]
    m = bb * tt
    x = x_ref[...].reshape(m, c)
    hb = _rms(x, g_ref[...]).astype(BF16)
    gate = jax.nn.gelu(_dot(hb, wy_ref[...]) + by_ref[...])
    xb = _dot(hb, wx_ref[...]) + bx_ref[...]

    us = []
    for s in range(bb):
        xs = xb[s * tt:(s + 1) * tt]
        us.append(_causal_conv(xs, tail_out[s], cw_ref[...], cb_ref[...]))
        tail_out[s] = xs[tt - SUBLANES:]
    u = us[0] if bb == 1 else jnp.concatenate(us, axis=0)

    ub = u.astype(BF16)
    sp = _softplus(-lam_ref[...])
    bw = c // LRU_BLOCKS
    for n in range(LRU_BLOCKS):
        sl = slice(n * bw, (n + 1) * bw)
        gg = _dot(ub[:, sl], wg_ref[n])
        r = jax.nn.sigmoid(gg[:, :bw] + bga_ref[:, sl])
        i = jax.nn.sigmoid(gg[:, bw:] + bgx_ref[:, sl])
        log_a = -LRU_C * r * sp[:, sl]
        a = jnp.exp(log_a)
        a_sc[:, sl] = a
        b_sc[:, sl] = jnp.sqrt(-jnp.tanh(log_a) * (a * a + 1.0)) * (i * u[:, sl])

    row = lax.broadcasted_iota(jnp.int32, (SUBLANES, c), 0)
    for s in range(bb):
        def group(gi, carry, s=s):
            r0 = pl.multiple_of(s * tt + gi * SUBLANES, SUBLANES)
            av = a_sc[pl.ds(r0, SUBLANES), :]
            bv = b_sc[pl.ds(r0, SUBLANES), :]
            for sh in (1, 2, 4):
                a_sh = jnp.where(row >= sh, pltpu.roll(av, sh, 0), 1.0)
                b_sh = jnp.where(row >= sh, pltpu.roll(bv, sh, 0), 0.0)
                bv = av * b_sh + bv
                av = av * a_sh
            hv = bv + av * carry
            b_sc[pl.ds(r0, SUBLANES), :] = hv
            return hv[SUBLANES - 1:]

        h_out[s] = lax.fori_loop(0, tt // SUBLANES, group, h_out[s])

    y = _dot((b_sc[...] * gate).astype(BF16), wo_ref[...]) + bo_ref[...]
    o_ref[...] = (x + y).reshape(bb, tt, c)


def _ffn_kernel(x_ref, tail_in, g_ref, wup_ref, cw_ref, cb_ref, wdn_ref, gf_ref,
                o_ref, tail_out, *, bb, tt, final_norm):
    @pl.when(pl.program_id(1) == 0)
    def _():
        tail_out[...] = tail_in[...]

    c = x_ref.shape[-1]
    m = bb * tt
    dff = wdn_ref.shape[0]
    x = x_ref[...].reshape(m, c)
    hb = _rms(x, g_ref[...]).astype(BF16)

    acc = jnp.zeros((m, c), F32)
    for j in range(dff // FFN_COL_CHUNK):
        halves = []
        for half in range(2):
            cols = slice(half * dff + j * FFN_COL_CHUNK, half * dff + (j + 1) * FFN_COL_CHUNK)
            up = _dot(hb, wup_ref[:, cols])
            parts = []
            for s in range(bb):
                xs = up[s * tt:(s + 1) * tt]
                parts.append(_causal_conv(xs, tail_out[s, :, cols], cw_ref[:, cols], cb_ref[:, cols]))
                tail_out[s, :, cols] = xs[tt - SUBLANES:]
            halves.append(parts[0] if bb == 1 else jnp.concatenate(parts, axis=0))
        act = jax.nn.gelu(halves[0]) * halves[1]
        acc = acc + _dot(act.astype(BF16), wdn_ref[j * FFN_COL_CHUNK:(j + 1) * FFN_COL_CHUNK, :])

    out = x + acc
    if final_norm:
        out = _rms(out, gf_ref[...])
    o_ref[...] = out.reshape(bb, tt, c)


def _blockdiag(z, head_masks):
    zb = z.astype(BF16)
    return jnp.concatenate([zb * head_masks[0], zb * head_masks[1]], axis=0)


def _seg_sum(z, ones_seg):
    zb = z.astype(BF16)
    width = ones_seg.shape[0]
    parts = [_dot(zb[:, p * width:(p + 1) * width], ones_seg) for p in range(z.shape[1] // width)]
    return jnp.concatenate(parts, axis=1)


def _rwkv_kernel(x_ref, tail_in, z_in, g_ref, mu_ref, wr_ref, wk_ref, wv_ref, wo_ref,
                 w0_ref, w1_ref, w2_ref, a0_ref, a1_ref, a2_ref, g1_ref, g2_ref,
                 kk_ref, ka_ref, rk_ref, lnw_ref, lnb_ref,
                 o_ref, tail_out, z_out,
                 bonus_sc, y_sc, rh_sc, ah_sc, bh_sc, kh_sc, vb_sc, tot_sc, *, bb, tt):
    @pl.when(pl.program_id(1) == 0)
    def _():
        tail_out[...] = tail_in[...]
        z_out[...] = z_in[...]

    c = x_ref.shape[-1]
    m = bb * tt
    npairs = c // PAIR
    x = x_ref[...].reshape(m, c)
    hn = _rms(x, g_ref[...])

    prevs = []
    for s in range(bb):
        hs = hn[s * tt:(s + 1) * tt]
        prevs.append(_shift_down(hs, 1, tail_out[s]))
        tail_out[s] = hs[tt - SUBLANES:]
    prev = prevs[0] if bb == 1 else jnp.concatenate(prevs, axis=0)
    xx = prev - hn
    mix = lambda j: (hn + xx * mu_ref[j:j + 1, :]).astype(BF16)

    r = _dot(mix(0), wr_ref[...])
    k = _dot(mix(2), wk_ref[...])
    v = _dot(mix(3), wv_ref[...])
    wl = w0_ref[...] + _dot(jnp.tanh(_dot(mix(1), w1_ref[...])).astype(BF16), w2_ref[...])
    lw = -jnp.exp(-_softplus(-wl) - 0.5)
    asig = jax.nn.sigmoid(a0_ref[...] + _dot(_dot(mix(4), a1_ref[...]).astype(BF16), a2_ref[...]))
    gate = _dot(jax.nn.sigmoid(_dot(mix(5), g1_ref[...])).astype(BF16), g2_ref[...])

    srow = lax.broadcasted_iota(jnp.int32, (2 * PAIR, 2 * PAIR), 0)
    scol = lax.broadcasted_iota(jnp.int32, (2 * PAIR, 2 * PAIR), 1)
    ones_seg = jnp.where(srow // HEAD == scol // HEAD, 1.0, 0.0).astype(BF16)

    kkr = k * kk_ref[...]
    kk = kkr / jnp.maximum(jnp.sqrt(_seg_sum(kkr * kkr, ones_seg)), 1e-12)
    k2 = k * (1.0 + (asig - 1.0) * ka_ref[...])

    tri = jnp.where(lax.broadcasted_iota(jnp.int32, (CHUNK, CHUNK), 0)
                    >= lax.broadcasted_iota(jnp.int32, (CHUNK, CHUNK), 1), 1.0, 0.0).astype(BF16)
    lw_hi, lw_lo = _split(lw)
    lp = jnp.concatenate([_dot(tri, lw_hi[i * CHUNK:(i + 1) * CHUNK]) + _dot(tri, lw_lo[i * CHUNK:(i + 1) * CHUNK])
                          for i in range(m // CHUNK)], axis=0)
    p_inv = jnp.exp(-lp)
    bonus_sc[...] = _seg_sum(r * k2 * rk_ref[...], ones_seg) * v
    rh_sc[...] = (r * jnp.exp(lp)).astype(BF16)
    ah_sc[...] = (-kk * jnp.exp(lp - lw)).astype(BF16)
    bh_sc[...] = (kk * asig * p_inv).astype(BF16)
    kh_sc[...] = (k2 * p_inv).astype(BF16)
    vb_sc[...] = v.astype(BF16)
    tot_row = lax.broadcasted_iota(jnp.int32, (TOT_ROWS, c), 0)
    for i in range(m // CHUNK):
        tot = lp[(i + 1) * CHUNK - 1:(i + 1) * CHUNK]
        tot_hi = tot.astype(BF16).astype(F32)
        blk = jnp.where(tot_row == 0, tot_hi, jnp.where(tot_row == 1, tot - tot_hi, 0.0))
        tot_sc[i * TOT_ROWS:(i + 1) * TOT_ROWS, :] = blk.astype(BF16)

    trow = lax.broadcasted_iota(jnp.int32, (CHUNK, PAIR), 0)
    tcol = lax.broadcasted_iota(jnp.int32, (CHUNK, PAIR), 1) % HEAD
    strict = trow > tcol
    incl = trow >= tcol
    eye_pk = jnp.where(trow == tcol, 1.0, 0.0)
    prow = lax.broadcasted_iota(jnp.int32, (PAIR, PAIR), 0)
    pcol = lax.broadcasted_iota(jnp.int32, (PAIR, PAIR), 1)
    same_head = (prow < HEAD) == (pcol < HEAD)
    lane = lax.broadcasted_iota(jnp.int32, (1, PAIR), 1)
    head_masks = (jnp.where(lane < HEAD, 1.0, 0.0).astype(BF16), jnp.where(lane >= HEAD, 1.0, 0.0).astype(BF16))
    bd = lambda t: _blockdiag(t, head_masks)
    ones_tot = jnp.ones((TOT_ROWS, PAIR), BF16)
    cps = tt // CHUNK
    group = 2 if bb % 2 == 0 else 1

    def step(it, carry):
        slab0 = it * group if cps == 1 else lax.div(it, cps) * group
        ci = 0 if cps == 1 else lax.rem(it, cps)
        streams = []
        for g in range(group):
            rows = pl.ds(pl.multiple_of((slab0 + g) * tt + ci * CHUNK, CHUNK), CHUNK)
            streams += [(slab0 + g, p, rows, slice(p * PAIR, (p + 1) * PAIR)) for p in range(npairs)]
        ns = range(len(streams))
        ld = lambda ref: [ref[rows, sl] for (_, _, rows, sl) in streams]
        rh, ahb, bhb, khb, vb = ld(rh_sc), ld(ah_sc), ld(bh_sc), ld(kh_sc), ld(vb_sc)
        tot = [tot_sc[pl.ds(pl.multiple_of(((slab0 + g) * cps + ci) * TOT_ROWS, TOT_ROWS), TOT_ROWS), :]
               for g in range(group)]
        sc = [_dot_nt(jnp.concatenate([ahb[s], rh[s]], axis=0),
                      jnp.concatenate([bd(bhb[s]), bd(khb[s])], axis=0)) for s in ns]
        col_decay = [jnp.exp(_dot_tn(tot[s // npairs][:, streams[s][3]], ones_tot)) for s in ns]
        a_ab = [jnp.where(strict, t[:CHUNK, :PAIR], 0.0) for t in sc]
        a_rb = [jnp.where(incl, t[CHUNK:, :PAIR], 0.0).astype(BF16) for t in sc]
        a_k = [jnp.concatenate([jnp.where(strict, t[:CHUNK, PAIR:], 0.0),
                                jnp.where(incl, t[CHUNK:, PAIR:], 0.0)], axis=0).astype(BF16) for t in sc]
        wv = [_dot(a_k[s], bd(vb[s])) for s in ns]
        ak = [_dot(t.astype(BF16), bd(t)) for t in a_ab]
        tm = [eye_pk + t for t in a_ab]
        for n in range(4):
            both = [_dot(jnp.concatenate([tm[s], ak[s]], axis=0).astype(BF16), bd(ak[s])) for s in ns]
            tm = [tm[s] + both[s][:CHUNK] for s in ns]
            ak = [t[CHUNK:] for t in both]
        tm = [tm[s] + _dot(tm[s].astype(BF16), bd(ak[s])) for s in ns]
        z = [z_out[slab, p] for (slab, p, _, _) in streams]
        zr = [_dot(jnp.concatenate([ahb[s], rh[s]], axis=0), z[s].astype(BF16)) for s in ns]
        u = [_dot(tm[s].astype(BF16), bd(zr[s][:CHUNK] + wv[s][:CHUNK])) for s in ns]
        upd = [_dot_tn(jnp.concatenate([bhb[s], khb[s]], axis=0),
                       jnp.concatenate([u[s].astype(BF16), vb[s]], axis=0)) for s in ns]
        for s, (slab, p, _, _) in enumerate(streams):
            z_out[slab, p] = col_decay[s] * (z[s] + jnp.where(same_head, upd[s], 0.0))
        y = [zr[s][CHUNK:] + wv[s][CHUNK:] + _dot(a_rb[s], bd(u[s])) for s in ns]
        for s, (_, _, rows, sl) in enumerate(streams):
            y_sc[rows, sl] = y[s]
        return carry

    lax.fori_loop(0, (bb // group) * cps, step, 0)

    y = y_sc[...]
    d = y - _seg_sum(y, ones_seg) * (1.0 / HEAD)
    var = _seg_sum(d * d, ones_seg) * (1.0 / HEAD)
    yn = d * lax.rsqrt(var + GN_EPS) * lnw_ref[...] + lnb_ref[...]
    out = _dot(((yn + bonus_sc[...]) * gate).astype(BF16), wo_ref[...])
    o_ref[...] = (x_ref[...].reshape(m, c) + out).reshape(bb, tt, c)


def _tile(batch, seq, rows=TILE_ROWS, max_tt=TILE_ROWS):
    tt = min(seq, max_tt)
    bb = max(1, min(batch, rows // tt))
    assert seq % tt == 0 and batch % bb == 0 and tt % CHUNK == 0
    return bb, tt


def _const_spec(arr):
    nd = arr.ndim
    return pl.BlockSpec(arr.shape, lambda b, t, _nd=nd: (0,) * _nd, pipeline_mode=pl.Buffered(1))


def _row_spec(bb, rows, width):
    return pl.BlockSpec((bb, rows, width), lambda b, t: (b, 0, 0))


def _params(vmem_bytes):
    return pltpu.CompilerParams(dimension_semantics=("arbitrary", "arbitrary"),
                                vmem_limit_bytes=min(int(vmem_bytes), VMEM_BUDGET_BYTES))


def _nbytes(*arrs):
    return sum(a.size * a.dtype.itemsize for a in arrs)


def _lru_layer(x, tail, h0, w):
    b, t, c = x.shape
    bb, tt = _tile(b, t)
    m = bb * tt
    x_spec = pl.BlockSpec((bb, tt, c), lambda i, j: (i, j, 0))
    consts = [w[n] for n in ("g", "wx", "bx", "wy", "by", "cw", "cb", "wg", "bga", "bgx", "lam", "wo", "bo")]
    vmem = _nbytes(*consts) + 4 * m * c * 4 + 12 * m * c * 4
    return pl.pallas_call(
        functools.partial(_lru_kernel, bb=bb, tt=tt),
        grid=(b // bb, t // tt),
        in_specs=[x_spec, _row_spec(bb, SUBLANES, c), _row_spec(bb, 1, c)] + [_const_spec(a) for a in consts],
        out_specs=[x_spec, _row_spec(bb, SUBLANES, c), _row_spec(bb, 1, c)],
        out_shape=[jax.ShapeDtypeStruct(x.shape, F32), jax.ShapeDtypeStruct((b, SUBLANES, c), F32),
                   jax.ShapeDtypeStruct((b, 1, c), F32)],
        scratch_shapes=[pltpu.VMEM((m, c), F32), pltpu.VMEM((m, c), F32)],
        compiler_params=_params(vmem),
        name="lru_layer",
    )(x, tail, h0, *consts)


def _ffn_layer(x, tail, w, final_norm):
    b, t, c = x.shape
    bb, tt = _tile(b, t)
    m = bb * tt
    dff2 = w["wup"].shape[1]
    x_spec = pl.BlockSpec((bb, tt, c), lambda i, j: (i, j, 0))
    consts = [w[n] for n in ("g", "wup", "cw", "cb", "wdn", "gf")]
    vmem = _nbytes(*consts) + 4 * m * c * 4 + 4 * bb * SUBLANES * dff2 * 4 + 8 * m * c * 4
    return pl.pallas_call(
        functools.partial(_ffn_kernel, bb=bb, tt=tt, final_norm=final_norm),
        grid=(b // bb, t // tt),
        in_specs=[x_spec, _row_spec(bb, SUBLANES, dff2)] + [_const_spec(a) for a in consts],
        out_specs=[x_spec, _row_spec(bb, SUBLANES, dff2)],
        out_shape=[jax.ShapeDtypeStruct(x.shape, F32), jax.ShapeDtypeStruct((b, SUBLANES, dff2), F32)],
        compiler_params=_params(vmem),
        name="ffn_final" if final_norm else "ffn_layer",
    )(x, tail, *consts)


def _rwkv_layer(x, tail, z0, w):
    b, t, c = x.shape
    bb, tt = _tile(b, t, TILE_ROWS if t >= TILE_ROWS else TILE_ROWS // 2, max_tt=TILE_ROWS // 2)
    m = bb * tt
    npairs = c // PAIR
    x_spec = pl.BlockSpec((bb, tt, c), lambda i, j: (i, j, 0))
    z_spec = pl.BlockSpec((bb, npairs, PAIR, PAIR), lambda i, j: (i, 0, 0, 0))
    consts = [w[n] for n in ("g", "mu", "wr", "wk", "wv", "wo", "w0", "w1", "w2", "a0", "a1", "a2",
                             "g1", "g2", "kk", "ka", "rk", "lnw", "lnb")]
    scratch = ([pltpu.VMEM((m, c), F32) for _ in range(2)] + [pltpu.VMEM((m, c), BF16) for _ in range(5)]
               + [pltpu.VMEM((m // CHUNK * TOT_ROWS, c), BF16)])
    vmem = _nbytes(*consts) + 4 * m * c * 4 + (2 * 4 + 5 * 2) * m * c + 16 * m * c * 4
    return pl.pallas_call(
        functools.partial(_rwkv_kernel, bb=bb, tt=tt),
        grid=(b // bb, t // tt),
        in_specs=[x_spec, _row_spec(bb, SUBLANES, c), z_spec] + [_const_spec(a) for a in consts],
        out_specs=[x_spec, _row_spec(bb, SUBLANES, c), z_spec],
        out_shape=[jax.ShapeDtypeStruct(x.shape, F32), jax.ShapeDtypeStruct((b, SUBLANES, c), F32),
                   jax.ShapeDtypeStruct((b, npairs, PAIR, PAIR), F32)],
        scratch_shapes=scratch,
        compiler_params=_params(vmem),
        name="rwkv_layer",
    )(x, tail, z0, *consts)


def _to_tail(buf):
    return jnp.pad(buf, ((0, 0), (SUBLANES - buf.shape[1], 0), (0, 0)))


def _wkv_to_pairs(s):
    b, h, n, _ = s.shape
    st = jnp.swapaxes(s, -1, -2).reshape(b, h // 2, 2, n, n)
    zero = jnp.zeros_like(st[:, :, 0])
    return jnp.concatenate([jnp.concatenate([st[:, :, 0], zero], axis=-1),
                            jnp.concatenate([zero, st[:, :, 1]], axis=-1)], axis=-2)


def _pairs_to_wkv(z):
    b, p, n2, _ = z.shape
    n = n2 // 2
    zt = jnp.swapaxes(z, -1, -2)
    return jnp.stack([zt[:, :, :n, :n], zt[:, :, n:, n:]], axis=2).reshape(b, 2 * p, n, n)


def _pad_cols(w, width):
    return jnp.pad(w, ((0, 0), (0, width - w.shape[1])))


def _pad_rows(w, height):
    return jnp.pad(w, ((0, height - w.shape[0]), (0, 0)))


def kernel(x_prompt, x_sample, state_lru_conv, state_lru_h, state_rwkv_shift, state_rwkv_wkv, state_ffn_conv,
           norm_mix, norm_ffn, norm_final,
           lru_w_x, lru_b_x, lru_w_y, lru_b_y, lru_conv_w, lru_conv_b, lru_ga_w, lru_ga_b, lru_gx_w, lru_gx_b,
           lru_lambda, lru_w_out, lru_b_out,
           rw_mu, rw_w_r, rw_w_k, rw_w_v, rw_w_o, rw_w0, rw_w1, rw_w2, rw_a0, rw_a1, rw_a2, rw_g1, rw_g2,
           rw_k_k, rw_k_a, rw_r_k, rw_ln_w, rw_ln_b,
           ffn_w_up, ffn_conv_w, ffn_conv_b, ffn_w_down):
    c = x_prompt.shape[-1]
    row = lambda v: v.reshape(1, -1).astype(F32)
    bf = lambda v: v.astype(BF16)

    lru_w = dict(
        g=row(norm_mix[0]), wx=bf(lru_w_x[0]), bx=row(lru_b_x[0]), wy=bf(lru_w_y[0]), by=row(lru_b_y[0]),
        cw=lru_conv_w[0], cb=row(lru_conv_b[0]),
        wg=bf(jnp.concatenate([lru_ga_w[0], lru_gx_w[0]], axis=-1)),
        bga=row(lru_ga_b[0]), bgx=row(lru_gx_b[0]), lam=row(lru_lambda[0]),
        wo=bf(lru_w_out[0]), bo=row(lru_b_out[0]))
    ffn_w = [dict(g=row(norm_ffn[l]), wup=bf(ffn_w_up[l]), cw=ffn_conv_w[l], cb=row(ffn_conv_b[l]),
                  wdn=bf(ffn_w_down[l]), gf=row(norm_final)) for l in range(2)]
    rw_w = dict(
        g=row(norm_mix[1]), mu=rw_mu[0], wr=bf(rw_w_r[0]), wk=bf(rw_w_k[0]), wv=bf(rw_w_v[0]), wo=bf(rw_w_o[0]),
        w0=row(rw_w0[0]), w1=bf(_pad_cols(rw_w1[0], LANES)), w2=bf(_pad_rows(rw_w2[0], LANES)),
        a0=row(rw_a0[0]), a1=bf(_pad_cols(rw_a1[0], LANES)), a2=bf(_pad_rows(rw_a2[0], LANES)),
        g1=bf(rw_g1[0]), g2=bf(rw_g2[0]),
        kk=row(rw_k_k[0]), ka=row(rw_k_a[0]), rk=row(rw_r_k[0]), lnw=row(rw_ln_w[0]), lnb=row(rw_ln_b[0]))

    def trunk(x, lru_conv, lru_h, rw_shift, rw_wkv, ffn_conv):
        x, lc, lh = _lru_layer(x, _to_tail(lru_conv[0]), lru_h[0][:, None, :], lru_w)
        x, fc0 = _ffn_layer(x, _to_tail(ffn_conv[0]), ffn_w[0], final_norm=False)
        x, rs, zz = _rwkv_layer(x, _to_tail(rw_shift[0][:, None, :]), _wkv_to_pairs(rw_wkv[0]), rw_w)
        y, fc1 = _ffn_layer(x, _to_tail(ffn_conv[1]), ffn_w[1], final_norm=True)
        return (y, lc[None, :, SUBLANES - (LRU_CONV - 1):], lh[None, :, 0], rs[None, :, SUBLANES - 1],
                _pairs_to_wkv(zz)[None],
                jnp.stack([fc0[:, SUBLANES - (FFN_CONV - 1):], fc1[:, SUBLANES - (FFN_CONV - 1):]]))

    bp = x_prompt.shape[0]
    zeros = lambda *shape: jnp.zeros(shape, F32)
    heads = state_rwkv_wkv.shape[2]
    prompt = trunk(x_prompt, zeros(1, bp, LRU_CONV - 1, c), zeros(1, bp, c), zeros(1, bp, c),
                   zeros(1, bp, heads, HEAD, HEAD), zeros(2, bp, FFN_CONV - 1, ffn_w_up.shape[-1]))
    sample = trunk(x_sample, state_lru_conv, state_lru_h, state_rwkv_shift, state_rwkv_wkv, state_ffn_conv)
    return (prompt[0], sample[0]) + prompt[1:] + sample[1:]
```

```python
import functools

import jax
import jax.numpy as jnp
from jax import lax
from jax.experimental import pallas as pl
from jax.experimental.pallas import tpu as pltpu

F32 = jnp.float32
BF16 = jnp.bfloat16

SUBLANES = 8
LANES = 128
VMEM_BUDGET_BYTES = 58 * 1024 * 1024

LRU_BLOCKS = 8
LRU_CONV = 4
LRU_C = 8.0
FFN_CONV = 3
HEAD = 64
PAIR = 2 * HEAD
CHUNK = 64
TOT_ROWS = 16
RMS_EPS = 1e-6
GN_EPS = 64e-5
TILE_ROWS = 512
OUT_ROW_BLOCKS = 2
FFN_COL_CHUNK = 1536
FFN_UP_PIECE = 1536


def _dot(a, b):
    return jnp.dot(a, b, preferred_element_type=F32)


def _dot_nt(a, b):
    return lax.dot_general(a, b, (((1,), (1,)), ((), ())), preferred_element_type=F32)


def _dot_tn(a, b):
    return lax.dot_general(a, b, (((0,), (0,)), ((), ())), preferred_element_type=F32)


def _split(x):
    hi = x.astype(BF16)
    lo = (x - hi.astype(F32)).astype(BF16)
    return hi, lo


def _rms(x, g):
    return x * lax.rsqrt(jnp.mean(x * x, axis=-1, keepdims=True) + RMS_EPS) * g


def _sigmoid(x):
    return 0.5 * jnp.tanh(0.5 * x) + 0.5


def _softplus(x):
    return jnp.maximum(x, 0.0) + jnp.log1p(jnp.exp(-jnp.abs(x)))


def _shift_down(x, k, tail):
    rolled = pltpu.roll(x, k, 0)
    head = pltpu.roll(tail, k, 0)
    row = lax.broadcasted_iota(jnp.int32, head.shape, 0)
    first = jnp.where(row < k, head, rolled[:SUBLANES])
    return jnp.concatenate([first, rolled[SUBLANES:]], axis=0)


def _causal_conv(xs, tail, w, b):
    k = w.shape[0]
    y = b + xs * w[k - 1:k]
    for j in range(k - 1):
        y = y + _shift_down(xs, k - 1 - j, tail) * w[j:j + 1]
    return y


def _lru_kernel(x_ref, tail_in, h_in, g_ref, wx_ref, bx_ref, wy_ref, by_ref, cw_ref, cb_ref,
                wg_ref, bga_ref, bgx_ref, lam_ref, wo_ref, bo_ref,
                o_ref, tail_out, h_out, *, bb, tt):
    @pl.when(pl.program_id(1) == 0)
    def _():
        tail_out[...] = tail_in[...]
        h_out[...] = h_in[...]

    c = x_ref.shape[-1]
    m = bb * tt
    x = x_ref[...].reshape(m, c)
    hb = _rms(x, g_ref[...]).astype(BF16)
    xb = _dot(hb, wx_ref[...]) + bx_ref[...]
    gate = jax.nn.gelu(_dot(hb, wy_ref[...]) + by_ref[...])

    us = []
    for s in range(bb):
        xs = xb[s * tt:(s + 1) * tt]
        us.append(_causal_conv(xs, tail_out[s], cw_ref[...], cb_ref[...]))
        tail_out[s] = xs[tt - SUBLANES:]
    u = us[0] if bb == 1 else jnp.concatenate(us, axis=0)

    ub = u.astype(BF16)
    sp = _softplus(-lam_ref[...])
    bw = c // LRU_BLOCKS
    a_parts, b_parts = [], []
    for n in range(LRU_BLOCKS):
        sl = slice(n * bw, (n + 1) * bw)
        gg = _dot(ub[:, sl], wg_ref[n])
        r = _sigmoid(gg[:, :bw] + bga_ref[:, sl])
        i = _sigmoid(gg[:, bw:] + bgx_ref[:, sl])
        log_a = -LRU_C * r * sp[:, sl]
        a = jnp.exp(log_a)
        a_parts.append(a)
        s1 = -jnp.tanh(log_a) * (a * a + 1.0)
        b_parts.append(jnp.where(s1 > 0.0, s1 * lax.rsqrt(s1), 0.0) * (i * u[:, sl]))

    a_all = jnp.concatenate(a_parts, axis=1)
    b_all = jnp.concatenate(b_parts, axis=1)
    row = lax.broadcasted_iota(jnp.int32, (SUBLANES, c), 0)
    keep = {sh: row >= sh for sh in (1, 2, 4)}
    hs = []
    for s in range(bb):
        carry = h_out[s]
        for gi in range(tt // SUBLANES):
            r0 = s * tt + gi * SUBLANES
            av = a_all[r0:r0 + SUBLANES]
            bv = b_all[r0:r0 + SUBLANES]
            for sh in (1, 2, 4):
                a_sh = jnp.where(keep[sh], pltpu.roll(av, sh, 0), 1.0)
                b_sh = jnp.where(keep[sh], pltpu.roll(bv, sh, 0), 0.0)
                bv = av * b_sh + bv
                av = av * a_sh
            hv = bv + av * carry
            hs.append(hv)
            carry = hv[SUBLANES - 1:]
        h_out[s] = carry

    gpb = len(hs) // OUT_ROW_BLOCKS
    ys = []
    for q in range(OUT_ROW_BLOCKS):
        rows = slice(q * gpb * SUBLANES, (q + 1) * gpb * SUBLANES)
        hq = jnp.concatenate(hs[q * gpb:(q + 1) * gpb], axis=0)
        ys.append(_dot((hq * gate[rows]).astype(BF16), wo_ref[...]) + bo_ref[...])
    o_ref[...] = (x + jnp.concatenate(ys, axis=0)).reshape(bb, tt, c)


def _ffn_kernel(x_ref, tail_in, g_ref, wup_ref, cw_ref, cb_ref, wdn_ref, gf_ref,
                o_ref, tail_out, *, bb, tt, final_norm):
    @pl.when(pl.program_id(1) == 0)
    def _():
        tail_out[...] = tail_in[...]

    c = x_ref.shape[-1]
    m = bb * tt
    dff = wdn_ref.shape[0]
    x = x_ref[...].reshape(m, c)
    hb = _rms(x, g_ref[...]).astype(BF16)

    acc = jnp.zeros((m, c), F32)
    for j in range(dff // FFN_COL_CHUNK):
        acts = []
        for piece in range(FFN_COL_CHUNK // FFN_UP_PIECE):
            halves = []
            for half in range(2):
                col0 = half * dff + j * FFN_COL_CHUNK + piece * FFN_UP_PIECE
                cols = slice(col0, col0 + FFN_UP_PIECE)
                up = _dot(hb, wup_ref[:, cols])
                parts = []
                for s in range(bb):
                    xs = up[s * tt:(s + 1) * tt]
                    parts.append(_causal_conv(xs, tail_out[s, :, cols], cw_ref[:, cols], cb_ref[:, cols]))
                    tail_out[s, :, cols] = xs[tt - SUBLANES:]
                halves.append(parts[0] if bb == 1 else jnp.concatenate(parts, axis=0))
            acts.append((jax.nn.gelu(halves[0]) * halves[1]).astype(BF16))
        act = acts[0] if len(acts) == 1 else jnp.concatenate(acts, axis=1)
        acc = acc + _dot(act, wdn_ref[j * FFN_COL_CHUNK:(j + 1) * FFN_COL_CHUNK, :])

    out = x + acc
    if final_norm:
        out = _rms(out, gf_ref[...])
    o_ref[...] = out.reshape(bb, tt, c)


def _blockdiag(z, head_masks):
    zb = z.astype(BF16)
    return jnp.concatenate([zb * head_masks[0], zb * head_masks[1]], axis=0)


def _seg_sum(z, ones_seg):
    zb = z.astype(BF16)
    width = ones_seg.shape[0]
    parts = [_dot(zb[:, p * width:(p + 1) * width], ones_seg) for p in range(z.shape[1] // width)]
    return jnp.concatenate(parts, axis=1)


def _rwkv_kernel(x_ref, tail_in, z_in, g_ref, mu_ref, wr_ref, wk_ref, wv_ref, wo_ref,
                 w0_ref, w1_ref, w2_ref, a0_ref, a1_ref, a2_ref, g1_ref, g2_ref,
                 kk_ref, ka_ref, rk_ref, lnw_ref, lnb_ref,
                 o_ref, tail_out, z_out,
                 bonus_sc, y_sc, rh_sc, ah_sc, bh_sc, kh_sc, vb_sc, tot_sc, *, bb, tt):
    @pl.when(pl.program_id(1) == 0)
    def _():
        tail_out[...] = tail_in[...]
        z_out[...] = z_in[...]

    c = x_ref.shape[-1]
    m = bb * tt
    npairs = c // PAIR
    x = x_ref[...].reshape(m, c)
    hn = _rms(x, g_ref[...])

    prevs = []
    for s in range(bb):
        hs = hn[s * tt:(s + 1) * tt]
        prevs.append(_shift_down(hs, 1, tail_out[s]))
        tail_out[s] = hs[tt - SUBLANES:]
    prev = prevs[0] if bb == 1 else jnp.concatenate(prevs, axis=0)
    xx = prev - hn
    mix = lambda j: (hn + xx * mu_ref[j:j + 1, :]).astype(BF16)

    w_mid = _dot(mix(1), w1_ref[...])
    a_mid = _dot(mix(4), a1_ref[...])
    g_mid = _dot(mix(5), g1_ref[...])
    k = _dot(mix(2), wk_ref[...])
    wl = w0_ref[...] + _dot(jnp.tanh(w_mid).astype(BF16), w2_ref[...])
    a_logit = a0_ref[...] + _dot(a_mid.astype(BF16), a2_ref[...])
    gate = _dot(_sigmoid(g_mid).astype(BF16), g2_ref[...])
    r = _dot(mix(0), wr_ref[...])
    v = _dot(mix(3), wv_ref[...])

    lw = -jnp.exp(-_softplus(-wl) - 0.5)
    tri = jnp.where(lax.broadcasted_iota(jnp.int32, (CHUNK, CHUNK), 0)
                    >= lax.broadcasted_iota(jnp.int32, (CHUNK, CHUNK), 1), 1.0, 0.0).astype(BF16)
    lw_hi, lw_lo = _split(lw)
    lp = jnp.concatenate([_dot(tri, lw_hi[i * CHUNK:(i + 1) * CHUNK]) + _dot(tri, lw_lo[i * CHUNK:(i + 1) * CHUNK])
                          for i in range(m // CHUNK)], axis=0)
    p_inv = jnp.exp(-lp)
    asig = _sigmoid(a_logit)

    srow = lax.broadcasted_iota(jnp.int32, (2 * PAIR, 2 * PAIR), 0)
    scol = lax.broadcasted_iota(jnp.int32, (2 * PAIR, 2 * PAIR), 1)
    ones_seg = jnp.where(srow // HEAD == scol // HEAD, 1.0, 0.0).astype(BF16)

    kkr = k * kk_ref[...]
    kk = kkr * lax.rsqrt(jnp.maximum(_seg_sum(kkr * kkr, ones_seg), 1e-24))
    k2 = k * (1.0 + (asig - 1.0) * ka_ref[...])

    bonus_sc[...] = _seg_sum(r * k2 * rk_ref[...], ones_seg) * v
    rh_sc[...] = (r * jnp.exp(lp)).astype(BF16)
    ah_sc[...] = (-kk * jnp.exp(lp - lw)).astype(BF16)
    bh_sc[...] = (kk * asig * p_inv).astype(BF16)
    kh_sc[...] = (k2 * p_inv).astype(BF16)
    vb_sc[...] = v.astype(BF16)
    tot_row = lax.broadcasted_iota(jnp.int32, (TOT_ROWS, c), 0)
    for i in range(m // CHUNK):
        tot = lp[(i + 1) * CHUNK - 1:(i + 1) * CHUNK]
        tot_hi = tot.astype(BF16).astype(F32)
        blk = jnp.where(tot_row == 0, tot_hi, jnp.where(tot_row == 1, tot - tot_hi, 0.0))
        tot_sc[i * TOT_ROWS:(i + 1) * TOT_ROWS, :] = blk.astype(BF16)

    trow = lax.broadcasted_iota(jnp.int32, (CHUNK, PAIR), 0)
    tcol = lax.broadcasted_iota(jnp.int32, (CHUNK, PAIR), 1) % HEAD
    strict = trow > tcol
    incl = trow >= tcol
    eye_pk = jnp.where(trow == tcol, 1.0, 0.0)
    prow = lax.broadcasted_iota(jnp.int32, (PAIR, PAIR), 0)
    pcol = lax.broadcasted_iota(jnp.int32, (PAIR, PAIR), 1)
    same_head = (prow < HEAD) == (pcol < HEAD)
    lane = lax.broadcasted_iota(jnp.int32, (1, PAIR), 1)
    head_masks = (jnp.where(lane < HEAD, 1.0, 0.0).astype(BF16), jnp.where(lane >= HEAD, 1.0, 0.0).astype(BF16))
    bd = lambda t: _blockdiag(t, head_masks)
    ones_tot = jnp.ones((TOT_ROWS, PAIR), BF16)
    cps = tt // CHUNK
    group = 2 if bb % 2 == 0 else 1

    def step(it, carry):
        slab0 = it * group if cps == 1 else lax.div(it, cps) * group
        ci = 0 if cps == 1 else lax.rem(it, cps)
        streams = []
        for g in range(group):
            rows = pl.ds(pl.multiple_of((slab0 + g) * tt + ci * CHUNK, CHUNK), CHUNK)
            streams += [(slab0 + g, p, rows, slice(p * PAIR, (p + 1) * PAIR)) for p in range(npairs)]
        ns = range(len(streams))
        ld = lambda ref: [ref[rows, sl] for (_, _, rows, sl) in streams]
        rh, ahb, bhb, khb, vb = ld(rh_sc), ld(ah_sc), ld(bh_sc), ld(kh_sc), ld(vb_sc)
        tot = [tot_sc[pl.ds(pl.multiple_of(((slab0 + g) * cps + ci) * TOT_ROWS, TOT_ROWS), TOT_ROWS), :]
               for g in range(group)]
        sc = [_dot_nt(jnp.concatenate([ahb[s], rh[s]], axis=0),
                      jnp.concatenate([bd(bhb[s]), bd(khb[s])], axis=0)) for s in ns]
        col_decay = [jnp.exp(_dot_tn(tot[s // npairs][:, streams[s][3]], ones_tot)) for s in ns]
        a_ab = [jnp.where(strict, t[:CHUNK, :PAIR], 0.0) for t in sc]
        a_rb = [jnp.where(incl, t[CHUNK:, :PAIR], 0.0).astype(BF16) for t in sc]
        a_k = [jnp.concatenate([jnp.where(strict, t[:CHUNK, PAIR:], 0.0),
                                jnp.where(incl, t[CHUNK:, PAIR:], 0.0)], axis=0).astype(BF16) for t in sc]
        wv = [_dot(a_k[s], bd(vb[s])) for s in ns]
        ak = [_dot(t.astype(BF16), bd(t)) for t in a_ab]
        tm = [eye_pk + t for t in a_ab]
        for n in range(4):
            both = [_dot(jnp.concatenate([tm[s], ak[s]], axis=0).astype(BF16), bd(ak[s])) for s in ns]
            tm = [tm[s] + both[s][:CHUNK] for s in ns]
            ak = [t[CHUNK:] for t in both]
        tm = [tm[s] + _dot(tm[s].astype(BF16), bd(ak[s])) for s in ns]
        z = [z_out[slab, p] for (slab, p, _, _) in streams]
        zr = [_dot(jnp.concatenate([ahb[s], rh[s]], axis=0), z[s].astype(BF16)) for s in ns]
        u = [_dot(tm[s].astype(BF16), bd(zr[s][:CHUNK] + wv[s][:CHUNK])) for s in ns]
        upd = [_dot_tn(jnp.concatenate([bhb[s], khb[s]], axis=0),
                       jnp.concatenate([u[s].astype(BF16), vb[s]], axis=0)) for s in ns]
        for s, (slab, p, _, _) in enumerate(streams):
            z_out[slab, p] = col_decay[s] * (z[s] + jnp.where(same_head, upd[s], 0.0))
        y = [zr[s][CHUNK:] + wv[s][CHUNK:] + _dot(a_rb[s], bd(u[s])) for s in ns]
        for s, (_, _, rows, sl) in enumerate(streams):
            y_sc[rows, sl] = y[s]
        return carry

    lax.fori_loop(0, (bb // group) * cps, step, 0)

    outs = []
    for q in range(OUT_ROW_BLOCKS):
        rows = slice(q * (m // OUT_ROW_BLOCKS), (q + 1) * (m // OUT_ROW_BLOCKS))
        y = y_sc[rows, :]
        d = y - _seg_sum(y, ones_seg) * (1.0 / HEAD)
        var = _seg_sum(d * d, ones_seg) * (1.0 / HEAD)
        yn = d * lax.rsqrt(var + GN_EPS) * lnw_ref[...] + lnb_ref[...]
        outs.append(_dot(((yn + bonus_sc[rows, :]) * gate[rows]).astype(BF16), wo_ref[...]))
    o_ref[...] = (x_ref[...].reshape(m, c) + jnp.concatenate(outs, axis=0)).reshape(bb, tt, c)


def _tile(batch, seq, rows=TILE_ROWS, max_tt=TILE_ROWS):
    tt = min(seq, max_tt)
    bb = max(1, min(batch, rows // tt))
    assert seq % tt == 0 and batch % bb == 0 and tt % CHUNK == 0
    return bb, tt


def _const_spec(arr):
    nd = arr.ndim
    return pl.BlockSpec(arr.shape, lambda b, t, _nd=nd: (0,) * _nd, pipeline_mode=pl.Buffered(1))


def _row_spec(bb, rows, width):
    return pl.BlockSpec((bb, rows, width), lambda b, t: (b, 0, 0))


def _params(vmem_bytes):
    return pltpu.CompilerParams(dimension_semantics=("arbitrary", "arbitrary"),
                                vmem_limit_bytes=min(int(vmem_bytes), VMEM_BUDGET_BYTES))


def _nbytes(*arrs):
    return sum(a.size * a.dtype.itemsize for a in arrs)


def _lru_layer(x, tail, h0, w):
    b, t, c = x.shape
    bb, tt = _tile(b, t)
    m = bb * tt
    x_spec = pl.BlockSpec((bb, tt, c), lambda i, j: (i, j, 0))
    consts = [w[n] for n in ("g", "wx", "bx", "wy", "by", "cw", "cb", "wg", "bga", "bgx", "lam", "wo", "bo")]
    vmem = _nbytes(*consts) + 4 * m * c * 4 + 12 * m * c * 4
    return pl.pallas_call(
        functools.partial(_lru_kernel, bb=bb, tt=tt),
        grid=(b // bb, t // tt),
        in_specs=[x_spec, _row_spec(bb, SUBLANES, c), _row_spec(bb, 1, c)] + [_const_spec(a) for a in consts],
        out_specs=[x_spec, _row_spec(bb, SUBLANES, c), _row_spec(bb, 1, c)],
        out_shape=[jax.ShapeDtypeStruct(x.shape, F32), jax.ShapeDtypeStruct((b, SUBLANES, c), F32),
                   jax.ShapeDtypeStruct((b, 1, c), F32)],
        compiler_params=_params(vmem),
        name="lru_layer",
    )(x, tail, h0, *consts)


def _ffn_layer(x, tail, w, final_norm):
    b, t, c = x.shape
    bb, tt = _tile(b, t)
    m = bb * tt
    dff2 = w["wup"].shape[1]
    x_spec = pl.BlockSpec((bb, tt, c), lambda i, j: (i, j, 0))
    consts = [w[n] for n in ("g", "wup", "cw", "cb", "wdn", "gf")]
    vmem = _nbytes(*consts) + 4 * m * c * 4 + 4 * bb * SUBLANES * dff2 * 4 + 8 * m * c * 4
    return pl.pallas_call(
        functools.partial(_ffn_kernel, bb=bb, tt=tt, final_norm=final_norm),
        grid=(b // bb, t // tt),
        in_specs=[x_spec, _row_spec(bb, SUBLANES, dff2)] + [_const_spec(a) for a in consts],
        out_specs=[x_spec, _row_spec(bb, SUBLANES, dff2)],
        out_shape=[jax.ShapeDtypeStruct(x.shape, F32), jax.ShapeDtypeStruct((b, SUBLANES, dff2), F32)],
        compiler_params=_params(vmem),
        name="ffn_final" if final_norm else "ffn_layer",
    )(x, tail, *consts)


def _rwkv_layer(x, tail, z0, w):
    b, t, c = x.shape
    bb, tt = _tile(b, t, TILE_ROWS if t >= TILE_ROWS else TILE_ROWS // 2, max_tt=TILE_ROWS // 2)
    m = bb * tt
    npairs = c // PAIR
    x_spec = pl.BlockSpec((bb, tt, c), lambda i, j: (i, j, 0))
    z_spec = pl.BlockSpec((bb, npairs, PAIR, PAIR), lambda i, j: (i, 0, 0, 0))
    consts = [w[n] for n in ("g", "mu", "wr", "wk", "wv", "wo", "w0", "w1", "w2", "a0", "a1", "a2",
                             "g1", "g2", "kk", "ka", "rk", "lnw", "lnb")]
    scratch = ([pltpu.VMEM((m, c), F32) for _ in range(2)] + [pltpu.VMEM((m, c), BF16) for _ in range(5)]
               + [pltpu.VMEM((m // CHUNK * TOT_ROWS, c), BF16)])
    vmem = _nbytes(*consts) + 4 * m * c * 4 + (2 * 4 + 5 * 2) * m * c + 16 * m * c * 4
    return pl.pallas_call(
        functools.partial(_rwkv_kernel, bb=bb, tt=tt),
        grid=(b // bb, t // tt),
        in_specs=[x_spec, _row_spec(bb, SUBLANES, c), z_spec] + [_const_spec(a) for a in consts],
        out_specs=[x_spec, _row_spec(bb, SUBLANES, c), z_spec],
        out_shape=[jax.ShapeDtypeStruct(x.shape, F32), jax.ShapeDtypeStruct((b, SUBLANES, c), F32),
                   jax.ShapeDtypeStruct((b, npairs, PAIR, PAIR), F32)],
        scratch_shapes=scratch,
        compiler_params=_params(vmem),
        name="rwkv_layer",
    )(x, tail, z0, *consts)


def _to_tail(buf):
    return jnp.pad(buf, ((0, 0), (SUBLANES - buf.shape[1], 0), (0, 0)))


def _wkv_to_pairs(s):
    b, h, n, _ = s.shape
    st = jnp.swapaxes(s, -1, -2).reshape(b, h // 2, 2, n, n)
    zero = jnp.zeros_like(st[:, :, 0])
    return jnp.concatenate([jnp.concatenate([st[:, :, 0], zero], axis=-1),
                            jnp.concatenate([zero, st[:, :, 1]], axis=-1)], axis=-2)


def _pairs_to_wkv(z):
    b, p, n2, _ = z.shape
    n = n2 // 2
    zt = jnp.swapaxes(z, -1, -2)
    return jnp.stack([zt[:, :, :n, :n], zt[:, :, n:, n:]], axis=2).reshape(b, 2 * p, n, n)


def _pad_cols(w, width):
    return jnp.pad(w, ((0, 0), (0, width - w.shape[1])))


def _pad_rows(w, height):
    return jnp.pad(w, ((0, height - w.shape[0]), (0, 0)))


def kernel(x_prompt, x_sample, state_lru_conv, state_lru_h, state_rwkv_shift, state_rwkv_wkv, state_ffn_conv,
           norm_mix, norm_ffn, norm_final,
           lru_w_x, lru_b_x, lru_w_y, lru_b_y, lru_conv_w, lru_conv_b, lru_ga_w, lru_ga_b, lru_gx_w, lru_gx_b,
           lru_lambda, lru_w_out, lru_b_out,
           rw_mu, rw_w_r, rw_w_k, rw_w_v, rw_w_o, rw_w0, rw_w1, rw_w2, rw_a0, rw_a1, rw_a2, rw_g1, rw_g2,
           rw_k_k, rw_k_a, rw_r_k, rw_ln_w, rw_ln_b,
           ffn_w_up, ffn_conv_w, ffn_conv_b, ffn_w_down):
    c = x_prompt.shape[-1]
    row = lambda v: v.reshape(1, -1).astype(F32)
    bf = lambda v: v.astype(BF16)

    lru_w = dict(
        g=row(norm_mix[0]), wx=bf(lru_w_x[0]), bx=row(lru_b_x[0]), wy=bf(lru_w_y[0]), by=row(lru_b_y[0]),
        cw=lru_conv_w[0], cb=row(lru_conv_b[0]),
        wg=bf(jnp.concatenate([lru_ga_w[0], lru_gx_w[0]], axis=-1)),
        bga=row(lru_ga_b[0]), bgx=row(lru_gx_b[0]), lam=row(lru_lambda[0]),
        wo=bf(lru_w_out[0]), bo=row(lru_b_out[0]))
    ffn_w = [dict(g=row(norm_ffn[l]), wup=bf(ffn_w_up[l]), cw=ffn_conv_w[l], cb=row(ffn_conv_b[l]),
                  wdn=bf(ffn_w_down[l]), gf=row(norm_final)) for l in range(2)]
    rw_w = dict(
        g=row(norm_mix[1]), mu=rw_mu[0], wr=bf(rw_w_r[0]), wk=bf(rw_w_k[0]), wv=bf(rw_w_v[0]), wo=bf(rw_w_o[0]),
        w0=row(rw_w0[0]), w1=bf(_pad_cols(rw_w1[0], LANES)), w2=bf(_pad_rows(rw_w2[0], LANES)),
        a0=row(rw_a0[0]), a1=bf(_pad_cols(rw_a1[0], LANES)), a2=bf(_pad_rows(rw_a2[0], LANES)),
        g1=bf(rw_g1[0]), g2=bf(rw_g2[0]),
        kk=row(rw_k_k[0]), ka=row(rw_k_a[0]), rk=row(rw_r_k[0]), lnw=row(rw_ln_w[0]), lnb=row(rw_ln_b[0]))

    def trunk(x, lru_conv, lru_h, rw_shift, rw_wkv, ffn_conv):
        x, lc, lh = _lru_layer(x, _to_tail(lru_conv[0]), lru_h[0][:, None, :], lru_w)
        x, fc0 = _ffn_layer(x, _to_tail(ffn_conv[0]), ffn_w[0], final_norm=False)
        x, rs, zz = _rwkv_layer(x, _to_tail(rw_shift[0][:, None, :]), _wkv_to_pairs(rw_wkv[0]), rw_w)
        y, fc1 = _ffn_layer(x, _to_tail(ffn_conv[1]), ffn_w[1], final_norm=True)
        return (y, lc[None, :, SUBLANES - (LRU_CONV - 1):], lh[None, :, 0], rs[None, :, SUBLANES - 1],
                _pairs_to_wkv(zz)[None],
                jnp.stack([fc0[:, SUBLANES - (FFN_CONV - 1):], fc1[:, SUBLANES - (FFN_CONV - 1):]]))

    bp = x_prompt.shape[0]
    zeros = lambda *shape: jnp.zeros(shape, F32)
    heads = state_rwkv_wkv.shape[2]
    prompt = trunk(x_prompt, zeros(1, bp, LRU_CONV - 1, c), zeros(1, bp, c), zeros(1, bp, c),
                   zeros(1, bp, heads, HEAD, HEAD), zeros(2, bp, FFN_CONV - 1, ffn_w_up.shape[-1]))
    sample = trunk(x_sample, state_lru_conv, state_lru_h, state_rwkv_shift, state_rwkv_wkv, state_ffn_conv)
    return (prompt[0], sample[0]) + prompt[1:] + sample[1:]
```
